```python
import math
import jax
import jax.numpy as jnp
from jax import lax
import numpy as np

D_MODEL = 1024
BATCH = 2
SEQ = 8192
DEPTH = 2

GRID_W = 64
CTX_LEN = 256
EPS = 1e-6
ROPE_BASE = 10000.0
QBLK = 128

HEAD_DIM = 64
SWA_HEADS = 4
SWA_KV_HEADS = 2
WINDOW = 128

DIFF_HEADS = 4
DIFF_QK_DIM = 32
DIFF_V_DIM = 64

MLA_HEADS = 4
MLA_Q_RANK = 192
MLA_KV_RANK = 128
MLA_NOPE = 64
MLA_ROPE = 32
MLA_V = 64

GDN_HEADS = 4
GDN_DK = 64
GDN_DV = 64
GDN_CONV = 5
GDN_CHUNK = 64
GDN_QKV = 2 * GDN_HEADS * GDN_DK + GDN_HEADS * GDN_DV

N_EXPERTS = 64
N_GROUPS = 8
TOPK_GROUPS = 4
TOP_K = 6
EXPERT_FF = 256
SHARED_FF = 256
ROUTED_SCALE = 2.5
MOE_BLOCK = 128

IN_SPLITS = (
    SWA_HEADS * HEAD_DIM,
    SWA_KV_HEADS * HEAD_DIM,
    SWA_KV_HEADS * HEAD_DIM,
    DIFF_HEADS * 2 * DIFF_QK_DIM,
    DIFF_HEADS * 2 * DIFF_QK_DIM,
    DIFF_HEADS * DIFF_V_DIM,
    MLA_Q_RANK,
    MLA_KV_RANK,
    MLA_ROPE,
    GDN_QKV,
    GDN_HEADS * GDN_DV,
    GDN_HEADS,
    GDN_HEADS,
    GDN_HEADS,
    GDN_HEADS,
)
IN_WIDTH = sum(IN_SPLITS)
MIX_WIDTH = SWA_HEADS * HEAD_DIM + DIFF_HEADS * DIFF_V_DIM + MLA_HEADS * MLA_V + GDN_HEADS * GDN_DV

kernel_name = 'hybrid_parallel_heads_moe_diffusion_block'


def rmsnorm(x, g):
    xf = x.astype(jnp.float32)
    y = xf * lax.rsqrt(jnp.mean(xf * xf, axis=-1, keepdims=True) + EPS)
    return (y * g.astype(jnp.float32)).astype(x.dtype)


def l2norm(x):
    xf = x.astype(jnp.float32)
    return (xf * lax.rsqrt(jnp.sum(xf * xf, axis=-1, keepdims=True) + EPS)).astype(x.dtype)


def axial_rope(rows, dim):
    quarter = dim // 4
    inv = ROPE_BASE ** (-jnp.arange(quarter, dtype=jnp.float32) / quarter)
    r = jnp.repeat(jnp.arange(rows, dtype=jnp.float32), GRID_W)
    col = jnp.tile(jnp.arange(GRID_W, dtype=jnp.float32), rows)
    ang = jnp.concatenate([r[:, None] * inv, col[:, None] * inv], axis=-1)
    return jnp.cos(ang), jnp.sin(ang)


def apply_rope(x, cos, sin):
    half = x.shape[-1] // 2
    shape = (x.shape[1],) + (1,) * (x.ndim - 3) + (half,)
    c = cos.reshape(shape).astype(x.dtype)
    s = sin.reshape(shape).astype(x.dtype)
    x1, x2 = x[..., :half], x[..., half:]
    return jnp.concatenate([x1 * c - x2 * s, x2 * c + x1 * s], axis=-1)


def sweep_query_blocks(fn, *qs):
    B, T = qs[0].shape[:2]
    nb = T // QBLK
    blocks = tuple(jnp.moveaxis(q.reshape((B, nb, QBLK) + q.shape[2:]), 1, 0) for q in qs)
    out = lax.map(lambda args: fn(*args), blocks)
    out = jnp.moveaxis(out, 0, 1)
    return out.reshape((B, T) + out.shape[3:])


def swa_attention(q, k, v, qc, kc, vc, sinks, need_ctx):
    B, S = q.shape[:2]
    C = kc.shape[1]
    G = SWA_HEADS // SWA_KV_HEADS
    nb = S // WINDOW
    scale = HEAD_DIM ** -0.5
    sink = sinks.astype(jnp.float32).reshape(SWA_KV_HEADS, G)[:, :, None, None]

    def band(t):
        tb = jnp.pad(t, ((0, 0), (WINDOW, WINDOW), (0, 0), (0, 0))).reshape(B, nb + 2, WINDOW, SWA_KV_HEADS, HEAD_DIM)
        return jnp.concatenate([tb[:, :-2], tb[:, 1:-1], tb[:, 2:]], axis=2)

    kw, vw = band(k), band(v)
    qb = q.reshape(B, nb, WINDOW, SWA_KV_HEADS, G, HEAD_DIM)
    qpos = jnp.arange(nb)[:, None] * WINDOW + jnp.arange(WINDOW)[None, :]
    kpos = jnp.arange(nb)[:, None] * WINDOW - WINDOW + jnp.arange(3 * WINDOW)[None, :]
    kp = kpos[:, None, :]
    valid = (jnp.abs(kp - qpos[:, :, None]) <= WINDOW) & (kp >= 0) & (kp < S)
    s_loc = jnp.einsum('bnqhgd,bnkhd->bnhgqk', qb, kw).astype(jnp.float32) * scale
    s_loc = jnp.where(valid[None, :, None, None], s_loc, -jnp.inf)
    s_ctx = jnp.einsum('bnqhgd,bkhd->bnhgqk', qb, kc).astype(jnp.float32) * scale
    s_sink = jnp.broadcast_to(sink, s_loc.shape[:-1] + (1,))
    p = jax.nn.softmax(jnp.concatenate([s_loc, s_ctx, s_sink], axis=-1), axis=-1)
    L = 3 * WINDOW
    p_loc = p[..., :L].astype(v.dtype)
    p_ctx = p[..., L:L + C].astype(v.dtype)
    o = jnp.einsum('bnhgqk,bnkhd->bnqhgd', p_loc, vw) + jnp.einsum('bnhgqk,bkhd->bnqhgd', p_ctx, vc)
    y = o.reshape(B, S, SWA_HEADS * HEAD_DIM)
    yc = None
    if need_ctx:
        qcg = qc.reshape(B, C, SWA_KV_HEADS, G, HEAD_DIM)
        s = jnp.einsum('bqhgd,bkhd->bhgqk', qcg, kc).astype(jnp.float32) * scale
        s = jnp.concatenate([s, jnp.broadcast_to(sink, s.shape[:-1] + (1,))], axis=-1)
        pc = jax.nn.softmax(s, axis=-1)[..., :C].astype(vc.dtype)
        yc = jnp.einsum('bhgqk,bkhd->bqhgd', pc, vc).reshape(B, C, SWA_HEADS * HEAD_DIM)
    return y, yc


def diff_attention(q, k, v, qc, kc, vc, lam, lam_init, norm_g, need_ctx):
    scale = DIFF_QK_DIM ** -0.5

    def make(keys, vals):
        def fn(qb):
            s = jnp.einsum('bqhmd,bkhmd->bhmqk', qb, keys).astype(jnp.float32) * scale
            p = jax.nn.softmax(s, axis=-1)
            p = (p[:, :, 0] - lam * p[:, :, 1]).astype(vals.dtype)
            return jnp.einsum('bhqk,bkhd->bqhd', p, vals)
        return fn

    def finish(o):
        o = rmsnorm(o, norm_g) * (1.0 - lam_init)
        return o.reshape(o.shape[:2] + (-1,))

    y = finish(sweep_query_blocks(make(jnp.concatenate([k, kc], axis=1), jnp.concatenate([v, vc], axis=1)), q))
    yc = finish(sweep_query_blocks(make(kc, vc), qc)) if need_ctx else None
    return y, yc


def mla_attention(cq, ckv, kr, cq_c, ckv_c, kr_c, gq, gkv, w_uq, w_ukv, rope, need_ctx):
    scale = (MLA_NOPE + MLA_ROPE) ** -0.5

    def queries(cq_, rotate):
        q = (rmsnorm(cq_, gq) @ w_uq).reshape(cq_.shape[:2] + (MLA_HEADS, MLA_NOPE + MLA_ROPE))
        qr = q[..., MLA_NOPE:]
        return q[..., :MLA_NOPE], (apply_rope(qr, *rope) if rotate else qr)

    def keys_values(ckv_, kr_, rotate):
        kv = (rmsnorm(ckv_, gkv) @ w_ukv).reshape(ckv_.shape[:2] + (MLA_HEADS, MLA_NOPE + MLA_V))
        return kv[..., :MLA_NOPE], (apply_rope(kr_, *rope) if rotate else kr_), kv[..., MLA_NOPE:]

    def make(kn_, kr_, v_):
        def fn(qn_b, qr_b):
            s = jnp.einsum('bqhd,bkhd->bhqk', qn_b, kn_) + jnp.einsum('bqhr,bkr->bhqk', qr_b, kr_)
            p = jax.nn.softmax(s.astype(jnp.float32) * scale, axis=-1).astype(v_.dtype)
            return jnp.einsum('bhqk,bkhd->bqhd', p, v_)
        return fn

    kn, krl, v = keys_values(ckv, kr, True)
    kn_c, krc, v_c = keys_values(ckv_c, kr_c, False)
    qn, qr = queries(cq, True)
    y = sweep_query_blocks(make(jnp.concatenate([kn, kn_c], axis=1), jnp.concatenate([krl, krc], axis=1),
                                jnp.concatenate([v, v_c], axis=1)), qn, qr)
    y = y.reshape(y.shape[:2] + (-1,))
    yc = None
    if need_ctx:
        qn_c, qr_c = queries(cq_c, False)
        yc = sweep_query_blocks(make(kn_c, krc, v_c), qn_c, qr_c)
        yc = yc.reshape(yc.shape[:2] + (-1,))
    return y, yc


def short_conv(x, w):
    pad = GDN_CONV // 2
    y = lax.conv_general_dilated(x, w[:, None, :].astype(x.dtype), window_strides=(1,), padding=[(pad, pad)],
                                 dimension_numbers=('NWC', 'WIO', 'NWC'), feature_group_count=x.shape[-1])
    return jax.nn.silu(y)


def gated_delta_chunked(q, k, v, g, beta, state):
    B, T, H, _ = q.shape
    dv = v.shape[-1]
    n = T // GDN_CHUNK
    Cc = GDN_CHUNK

    def to_chunks(t):
        t = jnp.moveaxis(t.astype(jnp.float32), 2, 1)
        return t.reshape((B, H, n, Cc) + t.shape[3:])

    qf, kf, vf = to_chunks(q), to_chunks(k), to_chunks(v)
    gc = jnp.cumsum(to_chunks(g), axis=-1)
    bt = to_chunks(beta)
    kb = kf * bt[..., None]
    vb = vf * bt[..., None]
    tri_incl = jnp.tril(jnp.ones((Cc, Cc), bool))
    tri_strict = jnp.tril(jnp.ones((Cc, Cc), bool), -1)
    diff = gc[..., :, None] - gc[..., None, :]
    decay = jnp.where(tri_incl, jnp.exp(jnp.where(tri_incl, diff, 0.0)), 0.0)
    a_mat = jnp.where(tri_strict, jnp.einsum('bhnid,bhnjd->bhnij', kb, kf) * decay, 0.0)
    eye = jnp.eye(Cc, dtype=jnp.float32)
    tmat = lax.linalg.triangular_solve(a_mat + eye, jnp.broadcast_to(eye, a_mat.shape), left_side=True, lower=True)
    eg = jnp.exp(gc)[..., None]
    u = tmat @ vb
    w = tmat @ (kb * eg)
    attn = jnp.einsum('bhnid,bhnjd->bhnij', qf, kf) * decay
    qg = qf * eg
    glast = gc[..., -1]
    kdec = kf * jnp.exp(glast[..., None] - gc)[..., None]

    def step(S, xs):
        qg_i, w_i, u_i, at_i, kd_i, gl_i = xs
        v_new = u_i - jnp.einsum('bhcd,bhde->bhce', w_i, S)
        o = jnp.einsum('bhcd,bhde->bhce', qg_i, S) + jnp.einsum('bhij,bhje->bhie', at_i, v_new)
        S = S * jnp.exp(gl_i)[..., None, None] + jnp.einsum('bhcd,bhce->bhde', kd_i, v_new)
        return S, o

    xs = tuple(jnp.moveaxis(t, 2, 0) for t in (qg, w, u, attn, kdec, glast))
    S, o = lax.scan(step, state, xs)
    o = jnp.moveaxis(jnp.moveaxis(o, 0, 2).reshape(B, H, T, dv), 1, 2)
    return o.astype(v.dtype), S


def gdn_mixer(qkv, z, af, ab, bf, bb, qkv_c, z_c, af_c, ab_c, bf_c, bb_c,
              conv_w, a_log_f, a_log_b, dtb_f, dtb_b, norm_g, need_ctx):
    f32 = jnp.float32

    def prep(qkv_, af_, ab_, bf_, bb_):
        B, T = qkv_.shape[:2]
        hcv = short_conv(qkv_, conv_w)
        q, k, v = jnp.split(hcv, [GDN_HEADS * GDN_DK, 2 * GDN_HEADS * GDN_DK], axis=-1)
        q = l2norm(q.reshape(B, T, GDN_HEADS, GDN_DK)) * (GDN_DK ** -0.5)
        k = l2norm(k.reshape(B, T, GDN_HEADS, GDN_DK))
        v = v.reshape(B, T, GDN_HEADS, GDN_DV)
        g_f = -jnp.exp(a_log_f.astype(f32)) * jax.nn.softplus(af_.astype(f32) + dtb_f.astype(f32))
        g_b = -jnp.exp(a_log_b.astype(f32)) * jax.nn.softplus(ab_.astype(f32) + dtb_b.astype(f32))
        return q, k, v, g_f, jax.nn.sigmoid(bf_.astype(f32)), g_b, jax.nn.sigmoid(bb_.astype(f32))

    q, k, v, gf, btf, gb, btb = prep(qkv, af, ab, bf, bb)
    qc, kc, vc, gfc, btfc, gbc, btbc = prep(qkv_c, af_c, ab_c, bf_c, bb_c)
    s0 = jnp.zeros((q.shape[0], GDN_HEADS, GDN_DK, GDN_DV), f32)
    flip = lambda t: jnp.flip(t, axis=1)
    oc_f, s_cf = gated_delta_chunked(qc, kc, vc, gfc, btfc, s0)
    oc_b, s_cb = gated_delta_chunked(flip(qc), flip(kc), flip(vc), flip(gbc), flip(btbc), s0)
    o_f, _ = gated_delta_chunked(q, k, v, gf, btf, s_cf)
    o_b, _ = gated_delta_chunked(flip(q), flip(k), flip(v), flip(gb), flip(btb), s_cb)

    def gated_out(o, zz):
        B, T = zz.shape[:2]
        o = rmsnorm(o, norm_g) * jax.nn.silu(zz.reshape(B, T, GDN_HEADS, GDN_DV))
        return o.reshape(B, T, GDN_HEADS * GDN_DV)

    y = gated_out(o_f + flip(o_b), z)
    yc = gated_out(oc_f + flip(oc_b), z_c) if need_ctx else None
    return y, yc


def moe_ffn(x2, w_router, bias, w_gate, w_up, w_down, sw_gate, sw_up, sw_down):
    T, D = x2.shape
    f32 = jnp.float32
    scores = jax.nn.sigmoid(x2.astype(f32) @ w_router.astype(f32))
    choice = scores + bias.astype(f32)
    per_group = N_EXPERTS // N_GROUPS
    grp_score = lax.top_k(choice.reshape(T, N_GROUPS, per_group), 2)[0].sum(-1)
    _, top_g = lax.top_k(grp_score, TOPK_GROUPS)
    gmask = jnp.any(top_g[:, :, None] == jnp.arange(N_GROUPS)[None, None, :], axis=1)
    emask = jnp.repeat(gmask, per_group, axis=1)
    _, idx = lax.top_k(jnp.where(emask, choice, -jnp.inf), TOP_K)
    wts = jnp.take_along_axis(scores, idx, axis=-1)
    wts = wts / jnp.sum(wts, axis=-1, keepdims=True) * ROUTED_SCALE

    A = T * TOP_K
    flat_e = idx.reshape(-1)
    flat_t = jnp.repeat(jnp.arange(T, dtype=jnp.int32), TOP_K)
    flat_w = wts.reshape(-1)
    order = jnp.argsort(flat_e)
    se, st, sw = flat_e[order], flat_t[order], flat_w[order]
    counts = jnp.zeros((N_EXPERTS,), jnp.int32).at[flat_e].add(1)
    start = jnp.cumsum(counts) - counts
    padded = (counts + MOE_BLOCK - 1) // MOE_BLOCK * MOE_BLOCK
    pad_end = jnp.cumsum(padded)
    pad_start = pad_end - padded
    dest = pad_start[se] + (jnp.arange(A, dtype=jnp.int32) - start[se])
    n_blocks = -(-(A + N_EXPERTS * (MOE_BLOCK - 1)) // MOE_BLOCK)
    buf_t = jnp.full((n_blocks * MOE_BLOCK,), T, jnp.int32).at[dest].set(st)
    buf_w = jnp.zeros((n_blocks * MOE_BLOCK,), f32).at[dest].set(sw)
    blk_e = jnp.minimum(jnp.searchsorted(pad_end, jnp.arange(n_blocks, dtype=jnp.int32) * MOE_BLOCK, side='right'),
                        N_EXPERTS - 1)
    xpad = jnp.concatenate([x2, jnp.zeros((1, D), x2.dtype)], axis=0)

    def body(y, blk):
        t_idx, wb, e = blk
        xb = xpad[t_idx]
        hb = jax.nn.silu(xb @ w_gate[e]) * (xb @ w_up[e])
        yb = (hb @ w_down[e]).astype(f32) * wb[:, None]
        return y.at[t_idx].add(yb), None

    y, _ = lax.scan(body, jnp.zeros((T + 1, D), f32),
                    (buf_t.reshape(n_blocks, MOE_BLOCK), buf_w.reshape(n_blocks, MOE_BLOCK), blk_e))
    shared = (jax.nn.silu(x2 @ sw_gate) * (x2 @ sw_up)) @ sw_down
    return y[:T].astype(x2.dtype) + shared


def mix_tokens(h, hc, rope_a, rope_b, rope_c, lam_init, w_in, w_out, sinks, lq1, lk1, lq2, lk2, diff_g,
               mla_gq, mla_gkv, w_uq, w_ukv, conv_w, a_log_f, a_log_b, dtb_f, dtb_b, gdn_g, need_ctx):
    f32 = jnp.float32
    cuts = [int(i) for i in np.cumsum(IN_SPLITS)[:-1]]
    (aq, ak, av, bq, bk, bv, cq, ckv, kr, dqkv, dz, daf, dab, dbf, dbb) = jnp.split(h @ w_in, cuts, axis=-1)
    (aq_c, ak_c, av_c, bq_c, bk_c, bv_c, cq_c, ckv_c, kr_c, dqkv_c, dz_c, daf_c, dab_c, dbf_c, dbb_c) = \
        jnp.split(hc @ w_in, cuts, axis=-1)
    heads = lambda t, *hs: t.reshape(t.shape[:2] + hs)

    y_a, yc_a = swa_attention(
        apply_rope(heads(aq, SWA_HEADS, HEAD_DIM), *rope_a),
        apply_rope(heads(ak, SWA_KV_HEADS, HEAD_DIM), *rope_a),
        heads(av, SWA_KV_HEADS, HEAD_DIM),
        heads(aq_c, SWA_HEADS, HEAD_DIM), heads(ak_c, SWA_KV_HEADS, HEAD_DIM), heads(av_c, SWA_KV_HEADS, HEAD_DIM),
        sinks, need_ctx)

    lam = (jnp.exp(jnp.sum(lq1.astype(f32) * lk1.astype(f32)))
           - jnp.exp(jnp.sum(lq2.astype(f32) * lk2.astype(f32))) + lam_init)
    y_b, yc_b = diff_attention(
        apply_rope(heads(bq, DIFF_HEADS, 2, DIFF_QK_DIM), *rope_b),
        apply_rope(heads(bk, DIFF_HEADS, 2, DIFF_QK_DIM), *rope_b),
        heads(bv, DIFF_HEADS, DIFF_V_DIM),
        heads(bq_c, DIFF_HEADS, 2, DIFF_QK_DIM), heads(bk_c, DIFF_HEADS, 2, DIFF_QK_DIM),
        heads(bv_c, DIFF_HEADS, DIFF_V_DIM), lam, lam_init, diff_g, need_ctx)

    y_c, yc_c = mla_attention(cq, ckv, kr, cq_c, ckv_c, kr_c, mla_gq, mla_gkv, w_uq, w_ukv, rope_c, need_ctx)

    y_d, yc_d = gdn_mixer(dqkv, dz, daf, dab, dbf, dbb, dqkv_c, dz_c, daf_c, dab_c, dbf_c, dbb_c,
                          conv_w, a_log_f, a_log_b, dtb_f, dtb_b, gdn_g, need_ctx)

    y = jnp.concatenate([y_a, y_b, y_c, y_d], axis=-1) @ w_out
    yc = (jnp.concatenate([yc_a, yc_b, yc_c, yc_d], axis=-1) @ w_out) if need_ctx else None
    return y, yc


def setup_inputs(seed: int = 0) -> dict:
    key = jax.random.key(seed)
    keys = iter(jax.random.split(key, 48))
    f32 = jnp.float32
    L, D = DEPTH, D_MODEL

    def normal(shape, scale):
        return jax.random.normal(next(keys), shape, f32) * scale

    def gain(shape):
        return 1.0 + 0.02 * jax.random.normal(next(keys), shape, f32)

    dt = jnp.exp(jax.random.uniform(next(keys), (L, 2, GDN_HEADS), f32, math.log(1e-3), math.log(1e-1)))
    dt_bias = dt + jnp.log(-jnp.expm1(-dt))
    a_log = jnp.log(jax.random.uniform(next(keys), (L, 2, GDN_HEADS), f32, 1.0, 16.0))
    return {
        'x': normal((BATCH, SEQ, D), 1.0),
        'c': normal((BATCH, D), 1.0),
        'ctx': normal((BATCH, CTX_LEN, D), 1.0),
        'c_ctx': normal((D,), 1.0),
        'w_ada': normal((L, D, 6 * D), 0.5 * D ** -0.5),
        'b_ada': normal((L, 6 * D), 0.01),
        'norm1_g': gain((L, D)),
        'norm2_g': gain((L, D)),
        'w_in': normal((L, D, IN_WIDTH), D ** -0.5),
        'w_out': normal((L, MIX_WIDTH, D), MIX_WIDTH ** -0.5),
        'swa_sinks': normal((L, SWA_HEADS), 0.5),
        'diff_lq1': normal((L, DIFF_QK_DIM), 0.1),
        'diff_lk1': normal((L, DIFF_QK_DIM), 0.1),
        'diff_lq2': normal((L, DIFF_QK_DIM), 0.1),
        'diff_lk2': normal((L, DIFF_QK_DIM), 0.1),
        'diff_norm_g': gain((L, DIFF_V_DIM)),
        'mla_q_norm_g': gain((L, MLA_Q_RANK)),
        'mla_kv_norm_g': gain((L, MLA_KV_RANK)),
        'mla_w_uq': normal((L, MLA_Q_RANK, MLA_HEADS * (MLA_NOPE + MLA_ROPE)), MLA_Q_RANK ** -0.5),
        'mla_w_ukv': normal((L, MLA_KV_RANK, MLA_HEADS * (MLA_NOPE + MLA_V)), MLA_KV_RANK ** -0.5),
        'gdn_conv_w': normal((L, GDN_CONV, GDN_QKV), GDN_CONV ** -0.5),
        'gdn_a_log_f': a_log[:, 0],
        'gdn_a_log_b': a_log[:, 1],
        'gdn_dt_bias_f': dt_bias[:, 0],
        'gdn_dt_bias_b': dt_bias[:, 1],
        'gdn_norm_g': gain((L, GDN_DV)),
        'moe_w_router': normal((L, D, N_EXPERTS), D ** -0.5),
        'moe_bias': normal((L, N_EXPERTS), 0.01),
        'moe_w_gate': normal((L, N_EXPERTS, D, EXPERT_FF), D ** -0.5),
        'moe_w_up': normal((L, N_EXPERTS, D, EXPERT_FF), D ** -0.5),
        'moe_w_down': normal((L, N_EXPERTS, EXPERT_FF, D), EXPERT_FF ** -0.5),
        'shared_w_gate': normal((L, D, SHARED_FF), D ** -0.5),
        'shared_w_up': normal((L, D, SHARED_FF), D ** -0.5),
        'shared_w_down': normal((L, SHARED_FF, D), SHARED_FF ** -0.5),
        'final_norm_g': gain((D,)),
    }


def reference(x, c, ctx, c_ctx, w_ada, b_ada, norm1_g, norm2_g, w_in, w_out, swa_sinks,
              diff_lq1, diff_lk1, diff_lq2, diff_lk2, diff_norm_g, mla_q_norm_g, mla_kv_norm_g,
              mla_w_uq, mla_w_ukv, gdn_conv_w, gdn_a_log_f, gdn_a_log_b, gdn_dt_bias_f, gdn_dt_bias_b,
              gdn_norm_g, moe_w_router, moe_bias, moe_w_gate, moe_w_up, moe_w_down,
              shared_w_gate, shared_w_up, shared_w_down, final_norm_g):
    B, S, D = x.shape
    rows = S // GRID_W
    rope_a = axial_rope(rows, HEAD_DIM)
    rope_b = axial_rope(rows, DIFF_QK_DIM)
    rope_c = axial_rope(rows, MLA_ROPE)
    s_lat = jax.nn.silu(c)
    s_ctx = jax.nn.silu(c_ctx)
    xc = ctx
    for l in range(DEPTH):
        last = l == DEPTH - 1
        lam_init = 0.8 - 0.6 * math.exp(-0.3 * l)
        mod = (s_lat @ w_ada[l] + b_ada[l]).reshape(B, 6, 1, D)
        modc = (s_ctx @ w_ada[l] + b_ada[l]).reshape(6, D)
        h = rmsnorm(x, norm1_g[l]) * (1.0 + mod[:, 1]) + mod[:, 0]
        hc = rmsnorm(xc, norm1_g[l]) * (1.0 + modc[1]) + modc[0]
        y, yc = mix_tokens(h, hc, rope_a, rope_b, rope_c, lam_init, w_in[l], w_out[l], swa_sinks[l],
                           diff_lq1[l], diff_lk1[l], diff_lq2[l], diff_lk2[l], diff_norm_g[l],
                           mla_q_norm_g[l], mla_kv_norm_g[l], mla_w_uq[l], mla_w_ukv[l],
                           gdn_conv_w[l], gdn_a_log_f[l], gdn_a_log_b[l], gdn_dt_bias_f[l], gdn_dt_bias_b[l],
                           gdn_norm_g[l], not last)
        x = x + mod[:, 2] * y
        h = rmsnorm(x, norm2_g[l]) * (1.0 + mod[:, 4]) + mod[:, 3]
        moe_args = (moe_w_router[l], moe_bias[l], moe_w_gate[l], moe_w_up[l], moe_w_down[l],
                    shared_w_gate[l], shared_w_up[l], shared_w_down[l])
        if last:
            f = moe_ffn(h.reshape(B * S, D), *moe_args).reshape(B, S, D)
        else:
            xc = xc + modc[2] * yc
            hc = rmsnorm(xc, norm2_g[l]) * (1.0 + modc[4]) + modc[3]
            f_all = moe_ffn(jnp.concatenate([h.reshape(B * S, D), hc.reshape(-1, D)], axis=0), *moe_args)
            f = f_all[:B * S].reshape(B, S, D)
            xc = xc + modc[5] * f_all[B * S:].reshape(xc.shape)
        x = x + mod[:, 5] * f
    return rmsnorm(x, final_norm_g)
```

```python
import functools
import math

import jax
import jax.numpy as jnp
from jax import lax
from jax.experimental import pallas as pl
from jax.experimental.pallas import tpu as pltpu

F32 = jnp.float32
BF16 = jnp.bfloat16
HIGHEST = lax.Precision.HIGHEST

GRID_W = 64
EPS = 1e-6
ROPE_BASE = 10000.0

HEAD_DIM = 64
SWA_HEADS = 4
SWA_KV_HEADS = 2
WINDOW = 128
DIFF_HEADS = 4
DIFF_QK_DIM = 32
DIFF_V_DIM = 64
MLA_HEADS = 4
MLA_Q_RANK = 192
MLA_KV_RANK = 128
MLA_NOPE = 64
MLA_ROPE = 32
MLA_V = 64
GDN_HEADS = 4
GDN_DK = 64
GDN_DV = 64
GDN_CONV = 5
GDN_CHUNK = 64
GDN_QKV = 2 * GDN_HEADS * GDN_DK + GDN_HEADS * GDN_DV
N_EXPERTS = 64
N_GROUPS = 8
TOPK_GROUPS = 4
TOP_K = 6
EXPERT_FF = 256
ROUTED_SCALE = 2.5

IN_SPLITS = (
    SWA_HEADS * HEAD_DIM, SWA_KV_HEADS * HEAD_DIM, SWA_KV_HEADS * HEAD_DIM,
    DIFF_HEADS * 2 * DIFF_QK_DIM, DIFF_HEADS * 2 * DIFF_QK_DIM, DIFF_HEADS * DIFF_V_DIM,
    MLA_Q_RANK, MLA_KV_RANK, MLA_ROPE,
    GDN_QKV, GDN_HEADS * GDN_DV, GDN_HEADS, GDN_HEADS, GDN_HEADS, GDN_HEADS,
)

TILE = 256
LANES = 128
NEG = -1e30
LOG2E = math.log2(math.e)
VMEM_LIMIT = 56 * 1024 * 1024

_AK, _AKS, _BK, _BKS, _CQ, _CKV, _KR, _KRS, _DQKV, _DZ, _DG, _NTOK = (
    0, 128, 256, 512, 768, 1024, 1152, 1280, 1408, 2176, 2432, 2560)
_AQ, _AQS, _AV, _BQ, _BQS, _BV, _NTRN = 0, 256, 512, 640, 896, 1152, 1408


def _cparams(sem):
    return pltpu.CompilerParams(dimension_semantics=sem, vmem_limit_bytes=VMEM_LIMIT)


def _nt_dot(a, b, precision=None):
    return lax.dot_general(a, b, (((1,), (1,)), ((), ())), preferred_element_type=F32, precision=precision)


def _tn_dot(a, b):
    return lax.dot_general(a, b, (((0,), (0,)), ((), ())), preferred_element_type=F32)


def _dot(a, b, precision=None):
    return jnp.dot(a, b, preferred_element_type=F32, precision=precision)


def _split_dot(x, m):
    hi = x.astype(BF16)
    lo = (x - hi.astype(F32)).astype(BF16)
    return _dot(hi, m) + _dot(lo, m)


def _msplit_dot(m, x):
    x1 = x.astype(BF16)
    r1 = x - x1.astype(F32)
    x2 = r1.astype(BF16)
    x3 = (r1 - x2.astype(F32)).astype(BF16)
    return _dot(m, x1) + _dot(m, x2) + _dot(m, x3)


def _silu(x):
    return x * jax.nn.sigmoid(x)


def _group_ones(n, group):
    shift = group.bit_length() - 1
    r = lax.shift_right_logical(lax.broadcasted_iota(jnp.int32, (n, n), 0), shift)
    c = lax.shift_right_logical(lax.broadcasted_iota(jnp.int32, (n, n), 1), shift)
    return jnp.where(r == c, 1.0, 0.0).astype(BF16)


def _ada_kernel(s_ref, w_ref, b_ref, o_ref):
    s = _silu(s_ref[...])
    o_ref[0] = _dot(s, w_ref[0], precision=HIGHEST) + b_ref[0]


def _ada_call(s8, w_ada, b_ada):
    L, D, D6 = w_ada.shape
    nj = D6 // D
    return pl.pallas_call(
        _ada_kernel,
        out_shape=jax.ShapeDtypeStruct((L, 8, D6), F32),
        grid=(L, nj),
        in_specs=[
            pl.BlockSpec((8, D), lambda l, j: (0, 0)),
            pl.BlockSpec((1, D, D), lambda l, j: (l, 0, j)),
            pl.BlockSpec((1, 1, D), lambda l, j: (l, 0, j)),
        ],
        out_specs=pl.BlockSpec((1, 8, D), lambda l, j: (l, 0, j)),
        compiler_params=_cparams(("arbitrary", "arbitrary")),
        name="ada_mod",
    )(s8, w_ada, b_ada.reshape(L, 1, D6))


def _in_kernel(scales, x_ref, mod_ref, g1_ref, wtok_ref, wtrn_ref, ctok_ref, stok_ref, ctrn_ref, strn_ref,
               gq_ref, gkv_ref, wuq_ref, wuqs_ref, wkn_ref, epl_ref, wvt_ref,
               ka_ref, qta_ref, vta_ref, kb_ref, qtb_ref, vtb_ref, kc_ref, qtc_ref, vtc_ref,
               dqkv_ref, dz_ref, dg_ref):
    sc_a, sc_b, sc_c = scales
    x = x_ref[0]
    mod = mod_ref[0]
    h = x * lax.rsqrt(jnp.mean(x * x, axis=-1, keepdims=True) + EPS) * g1_ref[...]
    h = h * (1.0 + mod[1:2]) + mod[0:1]
    hb = h.astype(BF16)

    def tok(a, b):
        return _dot(hb, wtok_ref[:, a:b])

    def trn(a, b):
        return _nt_dot(wtrn_ref[a:b, :], hb)

    ct = ctok_ref[...]
    st = stok_ref[...]
    ka_ref[0] = (tok(_AK, _AKS) * ct[:, 0:128] + tok(_AKS, _BK) * st[:, 0:128]).astype(BF16)
    qta_ref[0, 0] = ((trn(_AQ, _AQS) * ctrn_ref[0, 0:256] + trn(_AQS, _AV) * strn_ref[0, 0:256]) * sc_a).astype(BF16)
    vta_ref[0, 0] = trn(_AV, _BQ).astype(BF16)
    kb_ref[0] = (tok(_BK, _BKS) * ct[:, 128:384] + tok(_BKS, _CQ) * st[:, 128:384]).astype(BF16)
    qtb_ref[0, 0] = ((trn(_BQ, _BQS) * ctrn_ref[0, 256:512] + trn(_BQS, _BV) * strn_ref[0, 256:512]) * sc_b).astype(BF16)
    vtb_ref[0, 0] = trn(_BV, _NTRN).astype(BF16)
    cq = tok(_CQ, _CKV)
    cqn = (cq * lax.rsqrt(jnp.sum(cq * cq, axis=-1, keepdims=True) * (1.0 / MLA_Q_RANK) + EPS) * gq_ref[...]).astype(BF16)
    qtc = _nt_dot(wuq_ref[...], cqn) * ctrn_ref[0, 512:1024] + _nt_dot(wuqs_ref[...], cqn) * strn_ref[0, 512:1024]
    qtc_ref[0, 0] = (qtc * sc_c).astype(BF16)
    ckv = tok(_CKV, _KR)
    ckvn = (ckv * lax.rsqrt(jnp.mean(ckv * ckv, axis=-1, keepdims=True) + EPS) * gkv_ref[...]).astype(BF16)
    krr = (tok(_KR, _KRS) * ct[:, 384:512] + tok(_KRS, _DQKV) * st[:, 384:512]).astype(BF16)
    kc_ref[0] = (_dot(ckvn, wkn_ref[...]) + _dot(krr, epl_ref[...])).astype(BF16)
    vtc_ref[0, 0] = _nt_dot(wvt_ref[...], ckvn).astype(BF16)
    dqkv_ref[0] = tok(_DQKV, _DZ)
    dz_ref[0] = tok(_DZ, _DG)
    dg_ref[0] = tok(_DG, _NTOK)


def _in_call(xu, mod8, g1, lw, tabs, nct):
    B, T, D = xu.shape
    nt = T // TILE
    tm = TILE

    def full(a):
        nd = a.ndim
        return pl.BlockSpec(a.shape, lambda b, t: (0,) * nd)

    def tokspec(w):
        return pl.BlockSpec((1, tm, w), lambda b, t: (b, t, 0))

    def trnspec(r):
        return pl.BlockSpec((1, 1, r, tm), lambda b, t: (b, t, 0, 0))

    scales = (HEAD_DIM ** -0.5 * LOG2E, DIFF_QK_DIM ** -0.5 * LOG2E, (MLA_NOPE + MLA_ROPE) ** -0.5 * LOG2E)
    weights = (lw["wtok"], lw["wtrn"])
    tables = (tabs["ctok"], tabs["stok"], tabs["ctrn"], tabs["strn"])
    mla = (lw["gq"], lw["gkv"], lw["wuq"], lw["wuqs"], lw["wkn"], lw["epl"], lw["wvt"])
    in_specs = [
        tokspec(D),
        pl.BlockSpec((1, 6, D), lambda b, t: (jnp.where(t < nct, 0, b + 1), 0, 0)),
        full(g1), full(weights[0]), full(weights[1]),
        pl.BlockSpec((tm, 512), lambda b, t: (t, 0)),
        pl.BlockSpec((tm, 512), lambda b, t: (t, 0)),
        pl.BlockSpec((1, 1024, tm), lambda b, t: (t, 0, 0)),
        pl.BlockSpec((1, 1024, tm), lambda b, t: (t, 0, 0)),
    ] + [full(a) for a in mla]
    out_shape = (
        jax.ShapeDtypeStruct((B, T, 128), BF16), jax.ShapeDtypeStruct((B, nt, 256, tm), BF16),
        jax.ShapeDtypeStruct((B, nt, 128, tm), BF16),
        jax.ShapeDtypeStruct((B, T, 256), BF16), jax.ShapeDtypeStruct((B, nt, 256, tm), BF16),
        jax.ShapeDtypeStruct((B, nt, 256, tm), BF16),
        jax.ShapeDtypeStruct((B, T, 512), BF16), jax.ShapeDtypeStruct((B, nt, 512, tm), BF16),
        jax.ShapeDtypeStruct((B, nt, 256, tm), BF16),
        jax.ShapeDtypeStruct((B, T, GDN_QKV), F32), jax.ShapeDtypeStruct((B, T, 256), F32),
        jax.ShapeDtypeStruct((B, T, 128), F32),
    )
    out_specs = (
        tokspec(128), trnspec(256), trnspec(128),
        tokspec(256), trnspec(256), trnspec(256),
        tokspec(512), trnspec(512), trnspec(256),
        tokspec(GDN_QKV), tokspec(256), tokspec(128),
    )
    return pl.pallas_call(
        functools.partial(_in_kernel, scales),
        out_shape=out_shape, grid=(B, nt), in_specs=in_specs, out_specs=out_specs,
        compiler_params=_cparams(("arbitrary", "arbitrary")),
        name="in_proj",
    )(xu, mod8, g1, *weights, *tables, *mla)


def _swa_kernel(nt, q_ref, k0_ref, k1_ref, k2_ref, k3_ref, v0_ref, v1_ref, v2_ref, v3_ref, sink_ref,
                o_ref, yt_ref):
    t = pl.program_id(1)
    tq = TILE
    r = lax.broadcasted_iota(jnp.int32, (tq, tq), 0)
    q = lax.broadcasted_iota(jnp.int32, (tq, tq), 1)
    d = r - q
    zero = jnp.zeros((tq, tq), F32)
    neg = jnp.full((tq, tq), NEG, F32)

    def gate(cond):
        return jnp.where(cond, 0.0, NEG).astype(F32)

    bias = (
        None,
        jnp.where(d >= WINDOW, zero, neg) + gate(t >= 2),
        jnp.where(jnp.abs(d) <= WINDOW, zero, neg) + gate(t >= 1),
        jnp.where(d <= -WINDOW, zero, neg) + gate(jnp.logical_and(t >= 1, t <= nt - 2)),
    )
    krefs = (k0_ref, k1_ref, k2_ref, k3_ref)
    vrefs = (v0_ref, v1_ref, v2_ref, v3_ref)
    zq = jnp.zeros((HEAD_DIM, tq), BF16)
    group = SWA_HEADS // SWA_KV_HEADS
    for h in range(SWA_HEADS):
        g = h // group
        qh = q_ref[0, 0, HEAD_DIM * h:HEAD_DIM * (h + 1), :]
        wq = jnp.concatenate([qh, zq], axis=0) if g == 0 else jnp.concatenate([zq, qh], axis=0)
        sink = sink_ref[h:h + 1, :]
        s = []
        m = sink
        for j in range(4):
            sj = _dot(krefs[j][0], wq)
            if bias[j] is not None:
                sj = sj + bias[j]
            s.append(sj)
            m = jnp.maximum(m, jnp.max(sj, axis=0, keepdims=True))
        l = jnp.exp2(sink - m)
        acc = jnp.zeros((HEAD_DIM, tq), F32)
        for j in range(4):
            p = jnp.exp2(s[j] - m)
            l = l + jnp.sum(p, axis=0, keepdims=True)
            acc = acc + _dot(vrefs[j][0, 0, HEAD_DIM * g:HEAD_DIM * (g + 1), :], p.astype(BF16))
        yt_ref[HEAD_DIM * h:HEAD_DIM * (h + 1), :] = acc / l
    o_ref[0] = yt_ref[...].T.astype(BF16)


def _swa_call(qta, ka, vta, sink_b):
    B, nt, R, tm = qta.shape
    T = nt * tm

    def kspec(f):
        return pl.BlockSpec((1, tm, 128), lambda b, t: (b, f(t), 0))

    def vspec(f):
        return pl.BlockSpec((1, 1, 128, tm), lambda b, t: (b, f(t), 0, 0))

    fs = (lambda t: 0, lambda t: jnp.maximum(t - 1, 0), lambda t: t, lambda t: jnp.minimum(t + 1, nt - 1))
    return pl.pallas_call(
        functools.partial(_swa_kernel, nt),
        out_shape=jax.ShapeDtypeStruct((B, T, 256), BF16),
        grid=(B, nt),
        in_specs=[pl.BlockSpec((1, 1, R, tm), lambda b, t: (b, t, 0, 0))]
        + [kspec(f) for f in fs] + [vspec(f) for f in fs]
        + [pl.BlockSpec(sink_b.shape, lambda b, t: (0, 0))],
        out_specs=pl.BlockSpec((1, tm, 256), lambda b, t: (b, t, 0)),
        scratch_shapes=[pltpu.VMEM((256, tm), F32)],
        compiler_params=_cparams(("arbitrary", "arbitrary")),
        name="swa_attn",
    )(qta, ka, ka, ka, ka, vta, vta, vta, vta, sink_b)


def _flash_maps(n_maps, nchunks, get_k, get_wq, get_v, m_ref, l_ref, acc_ref):
    tq = TILE
    m_ref[...] = jnp.full(m_ref.shape, NEG, F32)
    l_ref[...] = jnp.zeros(l_ref.shape, F32)
    acc_ref[...] = jnp.zeros(acc_ref.shape, F32)

    def body(c, carry):
        for j in range(n_maps):
            s = _dot(get_k(c, j), get_wq(j))
            m_old = m_ref[j:j + 1, :]
            m_new = jnp.maximum(m_old, jnp.max(s, axis=0, keepdims=True))
            alpha = jnp.exp2(m_old - m_new)
            p = jnp.exp2(s - m_new)
            l_ref[j:j + 1, :] = alpha * l_ref[j:j + 1, :] + jnp.sum(p, axis=0, keepdims=True)
            acc_ref[j] = alpha * acc_ref[j] + _dot(get_v(c, j), p.astype(BF16))
            m_ref[j:j + 1, :] = m_new
        return carry

    lax.fori_loop(0, nchunks, body, 0)
    del tq


def _diff_kernel(nct, nt, lam_init, q_ref, k_ref, v_ref, lq1_ref, lk1_ref, lq2_ref, lk2_ref, g_ref,
                 o_ref, qm_ref, m_ref, l_ref, acc_ref, yt_ref):
    t = pl.program_id(1)
    tq = TILE
    n_maps = 2 * DIFF_HEADS
    rows = lax.broadcasted_iota(jnp.int32, (LANES, tq), 0)
    per = LANES // DIFF_QK_DIM
    for j in range(n_maps):
        grp, sub = j // per, j % per
        qg = q_ref[0, 0, LANES * grp:LANES * (grp + 1), :]
        keep = jnp.logical_and(rows >= DIFF_QK_DIM * sub, rows < DIFF_QK_DIM * (sub + 1))
        qm_ref[j] = jnp.where(keep, qg, jnp.zeros_like(qg))

    def get_k(c, j):
        grp = j // per
        return k_ref[0, pl.ds(pl.multiple_of(c * tq, tq), tq), LANES * grp:LANES * (grp + 1)]

    def get_wq(j):
        return qm_ref[j]

    def get_v(c, j):
        h = j // 2
        return v_ref[0, c, DIFF_V_DIM * h:DIFF_V_DIM * (h + 1), :]

    nchunks = jnp.where(t < nct, nct, nt)
    _flash_maps(n_maps, nchunks, get_k, get_wq, get_v, m_ref, l_ref, acc_ref)

    lam = (jnp.exp(jnp.sum(lq1_ref[...] * lk1_ref[...], axis=-1, keepdims=True))
           - jnp.exp(jnp.sum(lq2_ref[...] * lk2_ref[...], axis=-1, keepdims=True)) + lam_init)
    for h in range(DIFF_HEADS):
        o1 = acc_ref[2 * h] / l_ref[2 * h:2 * h + 1, :]
        o2 = acc_ref[2 * h + 1] / l_ref[2 * h + 1:2 * h + 2, :]
        o = o1 - lam * o2
        on = o * lax.rsqrt(jnp.mean(o * o, axis=0, keepdims=True) + EPS) * g_ref[...]
        yt_ref[DIFF_V_DIM * h:DIFF_V_DIM * (h + 1), :] = on * (1.0 - lam_init)
    o_ref[0] = yt_ref[...].T.astype(BF16)


def _diff_call(qtb, kb, vtb, lq1, lk1, lq2, lk2, g_b, lam_init, nct):
    B, nt, R, tm = qtb.shape
    T = nt * tm
    n_maps = 2 * DIFF_HEADS
    small = [lq1, lk1, lq2, lk2, g_b]
    return pl.pallas_call(
        functools.partial(_diff_kernel, nct, nt, lam_init),
        out_shape=jax.ShapeDtypeStruct((B, T, 256), BF16),
        grid=(B, nt),
        in_specs=[
            pl.BlockSpec((1, 1, R, tm), lambda b, t: (b, t, 0, 0)),
            pl.BlockSpec((1, T, 256), lambda b, t: (b, 0, 0)),
            pl.BlockSpec((1, nt, 256, tm), lambda b, t: (b, 0, 0, 0)),
        ] + [pl.BlockSpec(a.shape, lambda b, t: (0, 0)) for a in small],
        out_specs=pl.BlockSpec((1, tm, 256), lambda b, t: (b, t, 0)),
        scratch_shapes=[
            pltpu.VMEM((n_maps, LANES, tm), BF16),
            pltpu.VMEM((n_maps, tm), F32), pltpu.VMEM((n_maps, tm), F32),
            pltpu.VMEM((n_maps, DIFF_V_DIM, tm), F32),
            pltpu.VMEM((256, tm), F32),
        ],
        compiler_params=_cparams(("arbitrary", "arbitrary")),
        name="diff_attn",
    )(qtb, kb, vtb, *small)


def _mla_kernel(nct, nt, q_ref, k_ref, v_ref, o_ref, m_ref, l_ref, acc_ref, yt_ref):
    t = pl.program_id(1)
    tq = TILE

    def get_k(c, j):
        return k_ref[0, pl.ds(pl.multiple_of(c * tq, tq), tq), LANES * j:LANES * (j + 1)]

    def get_wq(j):
        return q_ref[0, 0, LANES * j:LANES * (j + 1), :]

    def get_v(c, j):
        return v_ref[0, c, MLA_V * j:MLA_V * (j + 1), :]

    nchunks = jnp.where(t < nct, nct, nt)
    _flash_maps(MLA_HEADS, nchunks, get_k, get_wq, get_v, m_ref, l_ref, acc_ref)
    for h in range(MLA_HEADS):
        yt_ref[MLA_V * h:MLA_V * (h + 1), :] = acc_ref[h] / l_ref[h:h + 1, :]
    o_ref[0] = yt_ref[...].T.astype(BF16)


def _mla_call(qtc, kc, vtc, nct):
    B, nt, R, tm = qtc.shape
    T = nt * tm
    return pl.pallas_call(
        functools.partial(_mla_kernel, nct, nt),
        out_shape=jax.ShapeDtypeStruct((B, T, 256), BF16),
        grid=(B, nt),
        in_specs=[
            pl.BlockSpec((1, 1, R, tm), lambda b, t: (b, t, 0, 0)),
            pl.BlockSpec((1, T, 512), lambda b, t: (b, 0, 0)),
            pl.BlockSpec((1, nt, 256, tm), lambda b, t: (b, 0, 0, 0)),
        ],
        out_specs=pl.BlockSpec((1, tm, 256), lambda b, t: (b, t, 0)),
        scratch_shapes=[
            pltpu.VMEM((8, tm), F32), pltpu.VMEM((8, tm), F32),
            pltpu.VMEM((MLA_HEADS, MLA_V, tm), F32),
            pltpu.VMEM((256, tm), F32),
        ],
        compiler_params=_cparams(("arbitrary", "arbitrary")),
        name="mla_attn",
    )(qtc, kc, vtc)


def _gdn_prep_kernel(nct, nt, x_ref, xp_ref, xn_ref, w_ref, alog_ref, dtb_ref, dg_ref,
                     q_ref, k_ref, v_ref, gb_ref, xe_ref):
    t = pl.program_id(1)
    tm = TILE
    pad = GDN_CONV // 2
    keep_prev = jnp.where(jnp.logical_or(t == 0, t == nct), 0.0, 1.0).astype(F32)
    keep_next = jnp.where(jnp.logical_or(t == nct - 1, t == nt - 1), 0.0, 1.0).astype(F32)
    xe_ref[0:8, :] = xp_ref[0] * keep_prev
    xe_ref[8:8 + tm, :] = x_ref[0]
    xe_ref[8 + tm:16 + tm, :] = xn_ref[0] * keep_next
    y = jnp.zeros((tm, GDN_QKV), F32)
    for j in range(GDN_CONV):
        y = y + w_ref[j:j + 1, :] * xe_ref[8 - pad + j:8 - pad + j + tm, :]
    y = _silu(y)
    hq = GDN_HEADS * GDN_DK
    ones = _group_ones(hq, GDN_DK)
    q = y[:, 0:hq]
    k = y[:, hq:2 * hq]
    q_ref[0] = q * lax.rsqrt(_split_dot(q * q, ones) + EPS) * (GDN_DK ** -0.5)
    k_ref[0] = k * lax.rsqrt(_split_dot(k * k, ones) + EPS)
    v_ref[0] = y[:, 2 * hq:]
    dg = dg_ref[0]
    lane = lax.broadcasted_iota(jnp.int32, dg.shape, 1)
    xs = dg + dtb_ref[...]
    softplus = jnp.maximum(xs, 0.0) + jnp.log1p(jnp.exp(-jnp.abs(xs)))
    gate = -jnp.exp(alog_ref[...]) * softplus
    gb_ref[0] = jnp.where(lane < 2 * GDN_HEADS, gate, jax.nn.sigmoid(dg))


def _gdn_prep_call(dqkv, dg, conv_w8, alog128, dtb128, nct):
    B, T, W = dqkv.shape
    nt = T // TILE
    tm = TILE
    hb = tm // 8
    return pl.pallas_call(
        functools.partial(_gdn_prep_kernel, nct, nt),
        out_shape=(jax.ShapeDtypeStruct((B, T, 256), F32),) * 3 + (jax.ShapeDtypeStruct((B, T, 128), F32),),
        grid=(B, nt),
        in_specs=[
            pl.BlockSpec((1, tm, W), lambda b, t: (b, t, 0)),
            pl.BlockSpec((1, 8, W), lambda b, t: (b, jnp.maximum(t * hb - 1, 0), 0)),
            pl.BlockSpec((1, 8, W), lambda b, t: (b, jnp.minimum((t + 1) * hb, T // 8 - 1), 0)),
            pl.BlockSpec((8, W), lambda b, t: (0, 0)),
            pl.BlockSpec((1, 128), lambda b, t: (0, 0)),
            pl.BlockSpec((1, 128), lambda b, t: (0, 0)),
            pl.BlockSpec((1, tm, 128), lambda b, t: (b, t, 0)),
        ],
        out_specs=(pl.BlockSpec((1, tm, 256), lambda b, t: (b, t, 0)),) * 3
        + (pl.BlockSpec((1, tm, 128), lambda b, t: (b, t, 0)),),
        scratch_shapes=[pltpu.VMEM((tm + 16, W), F32)],
        compiler_params=_cparams(("arbitrary", "arbitrary")),
        name="gdn_prep",
    )(dqkv, dqkv, dqkv, conv_w8, alog128, dtb128, dg)


def _gdn_chain(q, k, v, g_col, beta_col, s_old, incl, strict, last_row):
    cs = GDN_CHUNK
    inclb = incl.astype(BF16)
    gbc = jnp.broadcast_to(g_col, (cs, cs))
    gc = _msplit_dot(inclb, gbc)
    gdiff = _msplit_dot(inclb, gbc * strict)
    decay = jnp.exp(gdiff * incl) * incl
    eg = jnp.exp(gc)
    gl = gc[last_row:last_row + 1, :]
    beta = jnp.broadcast_to(beta_col, (cs, cs))
    kb = k * beta
    vb = v * beta
    kbf = k.astype(BF16)
    a = _nt_dot(kb.astype(BF16), kbf) * decay * strict
    eye = incl - strict
    tm_ = eye - a
    pw = a
    for _ in range(5):
        pwb = pw.astype(BF16)
        pw = _dot(pwb, pwb)
        tm_ = tm_ + _dot(tm_.astype(BF16), pw.astype(BF16))
    tb = tm_.astype(BF16)
    u = _dot(tb, vb.astype(BF16))
    w = _dot(tb, (kb * eg).astype(BF16))
    attn = _nt_dot(q.astype(BF16), kbf) * decay
    qg = q * eg
    kdec = k * jnp.exp(gl - gc)
    sb = s_old.astype(BF16)
    v_new = u - _dot(w.astype(BF16), sb)
    o = _dot(qg.astype(BF16), sb) + _dot(attn.astype(BF16), v_new.astype(BF16))
    s_new = s_old * jnp.exp(gl) + _tn_dot(kdec.astype(BF16), v_new.astype(BF16))
    return o, s_new


def _gdn_scan_kernel(qf_ref, kf_ref, vf_ref, gf_ref, qb_ref, kb_ref, vb_ref, gbk_ref,
                     of_ref, ob_ref, sf_ref, sb_ref):
    s = pl.program_id(1)
    cs = GDN_CHUNK

    @pl.when(s == 0)
    def _():
        sf_ref[...] = jnp.zeros(sf_ref.shape, F32)
        sb_ref[...] = jnp.zeros(sb_ref.shape, F32)

    r = lax.broadcasted_iota(jnp.int32, (cs, cs), 0)
    c = lax.broadcasted_iota(jnp.int32, (cs, cs), 1)
    one = jnp.ones((cs, cs), F32)
    zero = jnp.zeros((cs, cs), F32)
    dirs = (
        (qf_ref, kf_ref, vf_ref, gf_ref, of_ref, sf_ref, jnp.where(r >= c, one, zero), jnp.where(r > c, one, zero), cs - 1, 0),
        (qb_ref, kb_ref, vb_ref, gbk_ref, ob_ref, sb_ref, jnp.where(r <= c, one, zero), jnp.where(r < c, one, zero), 0, GDN_HEADS),
    )
    for q_ref, k_ref, v_ref, g_ref, o_ref, st_ref, incl, strict, last_row, goff in dirs:
        gates = g_ref[0]
        outs = []
        for h in range(GDN_HEADS):
            sl = slice(GDN_DK * h, GDN_DK * (h + 1))
            g_col = gates[:, goff + h:goff + h + 1]
            beta_col = gates[:, 2 * GDN_HEADS + goff + h:2 * GDN_HEADS + goff + h + 1]
            o, s_new = _gdn_chain(q_ref[0][:, sl], k_ref[0][:, sl], v_ref[0][:, sl], g_col, beta_col,
                                  st_ref[h], incl, strict, last_row)
            st_ref[h] = s_new
            outs.append(o)
        o_ref[0] = jnp.concatenate(outs, axis=-1)


def _gdn_scan_call(q, k, v, gb, nct):
    B, T, W = q.shape
    cs = GDN_CHUNK
    nc = T // cs
    ncc = nct * (TILE // cs)

    def fwd(b, s):
        return (b, s, 0)

    def bwd(b, s):
        return (b, jnp.where(s < ncc, ncc - 1 - s, nc - 1 - (s - ncc)), 0)

    specs = []
    for f in (fwd, bwd):
        specs += [pl.BlockSpec((1, cs, W), f)] * 3 + [pl.BlockSpec((1, cs, 128), f)]
    return pl.pallas_call(
        _gdn_scan_kernel,
        out_shape=(jax.ShapeDtypeStruct((B, T, W), F32),) * 2,
        grid=(B, nc),
        in_specs=specs,
        out_specs=(pl.BlockSpec((1, cs, W), fwd), pl.BlockSpec((1, cs, W), bwd)),
        scratch_shapes=[pltpu.VMEM((GDN_HEADS, GDN_DK, GDN_DV), F32)] * 2,
        compiler_params=_cparams(("arbitrary", "arbitrary")),
        name="gdn_scan",
    )(q, k, v, gb, q, k, v, gb)


def _route(logits_t, bias_col):
    ne, tm = logits_t.shape
    per = ne // N_GROUPS
    neg_inf = -jnp.inf
    scores = jax.nn.sigmoid(logits_t)
    choice = scores + bias_col
    idx_m = lax.broadcasted_iota(jnp.int32, (per, tm), 0).astype(F32)
    groups, gscore = [], []
    for g in range(N_GROUPS):
        cg = choice[per * g:per * (g + 1), :]
        m1 = jnp.max(cg, axis=0, keepdims=True)
        i1 = jnp.min(jnp.where(cg == m1, idx_m, float(per)), axis=0, keepdims=True)
        m2 = jnp.max(jnp.where(idx_m == i1, neg_inf, cg), axis=0, keepdims=True)
        groups.append(cg)
        gscore.append(m1 + m2)
    gsel = [jnp.zeros((1, tm), F32) for _ in range(N_GROUPS)]
    for _ in range(TOPK_GROUPS):
        gm = functools.reduce(jnp.maximum, gscore)
        found = jnp.zeros((1, tm), F32)
        for g in range(N_GROUPS):
            hit = jnp.logical_and(gscore[g] == gm, found < 0.5)
            gsel[g] = jnp.where(hit, 1.0, gsel[g])
            gscore[g] = jnp.where(hit, neg_inf, gscore[g])
            found = jnp.where(hit, 1.0, found)
    mc = jnp.concatenate([jnp.where(gsel[g] > 0.5, groups[g], neg_inf) for g in range(N_GROUPS)], axis=0)
    idx_e = lax.broadcasted_iota(jnp.int32, (ne, tm), 0).astype(F32)
    esel = jnp.zeros((ne, tm), F32)
    for _ in range(TOP_K):
        em = jnp.max(mc, axis=0, keepdims=True)
        ei = jnp.min(jnp.where(mc == em, idx_e, float(ne)), axis=0, keepdims=True)
        hit = idx_e == ei
        esel = jnp.where(hit, 1.0, esel)
        mc = jnp.where(hit, neg_inf, mc)
    picked = scores * esel
    return picked / jnp.sum(picked, axis=0, keepdims=True) * ROUTED_SCALE


def _out_kernel(ya_ref, yb_ref, yc_ref, of_ref, ob_ref, z_ref, x_ref, mod_ref, gd_ref, wo_ref, g2_ref,
                wr_ref, rb_ref, x1_ref, h2_ref, wt_ref):
    o = of_ref[0] + ob_ref[0]
    ones = _group_ones(GDN_HEADS * GDN_DV, GDN_DV)
    ms = _split_dot(o * o, ones) * (1.0 / GDN_DV)
    yd = o * lax.rsqrt(ms + EPS) * gd_ref[...] * _silu(z_ref[0])
    y = (_dot(ya_ref[0], wo_ref[0:256, :]) + _dot(yb_ref[0], wo_ref[256:512, :])
         + _dot(yc_ref[0], wo_ref[512:768, :]) + _dot(yd.astype(BF16), wo_ref[768:1024, :]))
    mod = mod_ref[0]
    x1 = x_ref[0] + mod[2:3] * y
    x1_ref[0] = x1
    h2 = x1 * lax.rsqrt(jnp.mean(x1 * x1, axis=-1, keepdims=True) + EPS) * g2_ref[...]
    h2 = h2 * (1.0 + mod[4:5]) + mod[3:4]
    h2_ref[0] = h2.astype(BF16)
    logits_t = _nt_dot(wr_ref[...], h2, precision=HIGHEST)
    w_t = _route(logits_t, rb_ref[...])
    w_pad = jnp.concatenate([w_t, jnp.zeros((LANES - N_EXPERTS, w_t.shape[1]), F32)], axis=0)
    wt_ref[0] = w_pad.T


def _out_call(ya, yb, yc, of, ob, dz, xu, mod8, gd, wo, g2, wr_t, rbias, nct):
    B, T, D = xu.shape
    nt = T // TILE
    tm = TILE

    def tokspec(w):
        return pl.BlockSpec((1, tm, w), lambda b, t: (b, t, 0))

    def full(a):
        nd = a.ndim
        return pl.BlockSpec(a.shape, lambda b, t: (0,) * nd)

    return pl.pallas_call(
        _out_kernel,
        out_shape=(jax.ShapeDtypeStruct((B, T, D), F32), jax.ShapeDtypeStruct((B, T, D), BF16),
                   jax.ShapeDtypeStruct((B, T, 128), F32)),
        grid=(B, nt),
        in_specs=[tokspec(256)] * 6 + [
            tokspec(D),
            pl.BlockSpec((1, 6, D), lambda b, t: (jnp.where(t < nct, 0, b + 1), 0, 0)),
            full(gd), full(wo), full(g2), full(wr_t), full(rbias),
        ],
        out_specs=(tokspec(D), tokspec(D), tokspec(128)),
        compiler_params=_cparams(("arbitrary", "arbitrary")),
        name="out_proj_route",
    )(ya, yb, yc, of, ob, dz, xu, mod8, gd, wo, g2, wr_t, rbias)


def _moe_kernel(last, n_ctx_rows, x1_ref, h_ref, wt_ref, modc_ref, modl_ref, wg_ref, wu_ref, wd_ref,
                sg_ref, su_ref, sd_ref, gf_ref, o_ref, acc_ref):
    i = pl.program_id(1)
    e = pl.program_id(2)
    tm = h_ref.shape[1]

    @pl.when(e == 0)
    def _():
        acc_ref[...] = jnp.zeros(acc_ref.shape, F32)

    @pl.when(e < N_EXPERTS)
    def _():
        x = h_ref[0]
        wt = wt_ref[0]
        lane = lax.broadcasted_iota(jnp.int32, wt.shape, 1)
        col = jnp.sum(jnp.where(lane == e, wt, 0.0), axis=-1, keepdims=True)
        hh = _silu(_dot(x, wg_ref[0])) * _dot(x, wu_ref[0]) * col
        acc_ref[...] += _dot(hh.astype(BF16), wd_ref[0])

    @pl.when(e == N_EXPERTS)
    def _():
        x = h_ref[0]
        hh = _silu(_dot(x, sg_ref[...])) * _dot(x, su_ref[...])
        f = acc_ref[...] + _dot(hh.astype(BF16), sd_ref[...])
        row = lax.broadcasted_iota(jnp.int32, (tm, 1), 0) + i * tm
        gate = jnp.where(row < n_ctx_rows, modc_ref[0][5:6], modl_ref[0][5:6])
        x2 = x1_ref[0] + gate * f
        if last:
            x2 = x2 * lax.rsqrt(jnp.mean(x2 * x2, axis=-1, keepdims=True) + EPS) * gf_ref[...]
        o_ref[0] = x2


def _moe_tile(T):
    best = LANES
    for cand in range(LANES, 1408 + 1, LANES):
        if T % cand == 0:
            best = cand
    return best


def _moe_call(x1, h2, wt, mod8, wg, wu, wd, sg, su, sd, gfin, last, n_ctx_rows):
    B, T, D = x1.shape
    tm = _moe_tile(T)
    F = wg.shape[-1]

    def tokspec(w):
        return pl.BlockSpec((1, tm, w), lambda b, i, e: (b, i, 0))

    def full(a):
        nd = a.ndim
        return pl.BlockSpec(a.shape, lambda b, i, e: (0,) * nd)

    def espec(shape):
        return pl.BlockSpec((1,) + shape, lambda b, i, e: (jnp.minimum(e, N_EXPERTS - 1), 0, 0))

    return pl.pallas_call(
        functools.partial(_moe_kernel, last, n_ctx_rows),
        out_shape=jax.ShapeDtypeStruct((B, T, D), F32),
        grid=(B, T // tm, N_EXPERTS + 1),
        in_specs=[
            tokspec(D), tokspec(D), tokspec(128),
            pl.BlockSpec((1, 6, D), lambda b, i, e: (0, 0, 0)),
            pl.BlockSpec((1, 6, D), lambda b, i, e: (b + 1, 0, 0)),
            espec((D, F)), espec((D, F)), espec((F, D)),
            full(sg), full(su), full(sd), full(gfin),
        ],
        out_specs=tokspec(D),
        scratch_shapes=[pltpu.VMEM((tm, D), F32)],
        compiler_params=_cparams(("arbitrary", "arbitrary", "arbitrary")),
        name="moe_dense",
    )(x1, h2, wt, mod8, mod8, wg, wu, wd, sg, su, sd, gfin)


def _swap_cols(w, dim):
    d0, n = w.shape[0], w.shape[1] // dim
    w4 = w.reshape(d0, n, 2, dim // 2)
    return jnp.concatenate([-w4[:, :, 1:], w4[:, :, :1]], axis=2).reshape(d0, n * dim)


def _pad_cols(w, width):
    return jnp.pad(w, ((0, 0), (0, width - w.shape[1])))


def _layer_weights(w_in, gq, gkv, w_uq, w_ukv):
    cuts, acc = [], 0
    for s in IN_SPLITS[:-1]:
        acc += s
        cuts.append(acc)
    (aq, ak, av, bq, bk, bv, cq, ckv, kr, dqkv, dz, daf, dab, dbf, dbb) = jnp.split(w_in, cuts, axis=-1)
    wtok = jnp.concatenate([
        ak, _swap_cols(ak, HEAD_DIM), bk, _swap_cols(bk, DIFF_QK_DIM),
        _pad_cols(cq, 256), ckv, _pad_cols(kr, 128), _pad_cols(_swap_cols(kr, MLA_ROPE), 128),
        dqkv, dz, _pad_cols(jnp.concatenate([daf, dab, dbf, dbb], axis=-1), 128),
    ], axis=-1).astype(BF16)
    wtrn = jnp.concatenate([
        aq, _swap_cols(aq, HEAD_DIM), av, bq, _swap_cols(bq, DIFF_QK_DIM), bv,
    ], axis=-1).T.astype(BF16)
    qd = MLA_NOPE + MLA_ROPE
    uq = w_uq.reshape(MLA_Q_RANK, MLA_HEADS, qd)
    uq_n, uq_r = uq[:, :, :MLA_NOPE], uq[:, :, MLA_NOPE:]
    uq_rs = jnp.concatenate([-uq_r[:, :, MLA_ROPE // 2:], uq_r[:, :, :MLA_ROPE // 2]], axis=-1)
    zpad = jnp.zeros((MLA_Q_RANK, MLA_HEADS, LANES - qd), F32)
    wuq = jnp.concatenate([uq_n, uq_r, zpad], axis=-1).reshape(MLA_Q_RANK, MLA_HEADS * LANES)
    wuqs = jnp.concatenate([jnp.zeros_like(uq_n), uq_rs, zpad], axis=-1).reshape(MLA_Q_RANK, MLA_HEADS * LANES)
    wuq = jnp.pad(wuq.T, ((0, 0), (0, 256 - MLA_Q_RANK))).astype(BF16)
    wuqs = jnp.pad(wuqs.T, ((0, 0), (0, 256 - MLA_Q_RANK))).astype(BF16)
    ukv = w_ukv.reshape(MLA_KV_RANK, MLA_HEADS, MLA_NOPE + MLA_V)
    wkn = jnp.concatenate([ukv[:, :, :MLA_NOPE], jnp.zeros((MLA_KV_RANK, MLA_HEADS, LANES - MLA_NOPE), F32)],
                          axis=-1).reshape(MLA_KV_RANK, MLA_HEADS * LANES).astype(BF16)
    wvt = ukv[:, :, MLA_NOPE:].reshape(MLA_KV_RANK, MLA_HEADS * MLA_V).T.astype(BF16)
    rr = jnp.arange(LANES)[:, None]
    cc = jnp.arange(MLA_HEADS * LANES)[None, :]
    epl = jnp.where(jnp.logical_and(rr < MLA_ROPE, (cc % LANES) == rr + MLA_NOPE), 1.0, 0.0).astype(BF16)
    return dict(wtok=wtok, wtrn=wtrn, gq=_pad_cols(gq[None, :], 256), gkv=gkv[None, :],
                wuq=wuq, wuqs=wuqs, wkn=wkn, epl=epl, wvt=wvt)


def _rope_full(S, C, dim):
    quarter = dim // 4
    inv = ROPE_BASE ** (-jnp.arange(quarter, dtype=F32) / quarter)
    rows = S // GRID_W
    r = jnp.repeat(jnp.arange(rows, dtype=F32), GRID_W)
    col = jnp.tile(jnp.arange(GRID_W, dtype=F32), rows)
    ang = jnp.concatenate([r[:, None] * inv, col[:, None] * inv], axis=-1)
    cos = jnp.concatenate([jnp.ones((C, dim // 2), F32), jnp.cos(ang)], axis=0)
    sin = jnp.concatenate([jnp.zeros((C, dim // 2), F32), jnp.sin(ang)], axis=0)
    return jnp.concatenate([cos, cos], axis=-1), jnp.concatenate([sin, sin], axis=-1)


def _rope_tables(S, C):
    T = S + C
    nt = T // TILE
    ca, sa = _rope_full(S, C, HEAD_DIM)
    cb, sb = _rope_full(S, C, DIFF_QK_DIM)
    cr, sr = _rope_full(S, C, MLA_ROPE)
    ctok = jnp.concatenate([jnp.tile(ca, (1, 2)), jnp.tile(cb, (1, 8)), _pad_cols(cr, 128)], axis=-1)
    stok = jnp.concatenate([jnp.tile(sa, (1, 2)), jnp.tile(sb, (1, 8)), _pad_cols(sr, 128)], axis=-1)
    one = jnp.ones((T, MLA_NOPE), F32)
    zero = jnp.zeros((T, MLA_NOPE), F32)
    padz = jnp.zeros((T, LANES - MLA_NOPE - MLA_ROPE), F32)
    cmla = jnp.tile(jnp.concatenate([one, cr, padz], axis=-1), (1, MLA_HEADS))
    smla = jnp.tile(jnp.concatenate([zero, sr, padz], axis=-1), (1, MLA_HEADS))
    ctrn = jnp.concatenate([jnp.tile(ca, (1, 4)), jnp.tile(cb, (1, 8)), cmla], axis=-1)
    strn = jnp.concatenate([jnp.tile(sa, (1, 4)), jnp.tile(sb, (1, 8)), smla], axis=-1)

    def blocked_t(a):
        return a.reshape(nt, TILE, a.shape[1]).transpose(0, 2, 1)

    return dict(ctok=ctok, stok=stok, ctrn=blocked_t(ctrn), strn=blocked_t(strn))


def kernel(x, c, ctx, c_ctx, w_ada, b_ada, norm1_g, norm2_g, w_in, w_out, swa_sinks, diff_lq1, diff_lk1, diff_lq2, diff_lk2, diff_norm_g, mla_q_norm_g, mla_kv_norm_g, mla_w_uq, mla_w_ukv, gdn_conv_w, gdn_a_log_f, gdn_a_log_b, gdn_dt_bias_f, gdn_dt_bias_b, gdn_norm_g, moe_w_router, moe_bias, moe_w_gate, moe_w_up, moe_w_down, shared_w_gate, shared_w_up, shared_w_down, final_norm_g):
    B, S, D = x.shape
    C = ctx.shape[1]
    L = w_ada.shape[0]
    assert S % TILE == 0 and C % TILE == 0 and B + 1 <= 8 and S % GRID_W == 0
    T = S + C
    nct = C // TILE

    xu = jnp.concatenate([ctx, x], axis=1)
    s8 = jnp.zeros((8, D), F32).at[0].set(c_ctx).at[1:B + 1].set(c)
    mod_all = _ada_call(s8, w_ada, b_ada).reshape(L, 8, 6, D)
    tabs = _rope_tables(S, C)

    for l in range(L):
        last = l == L - 1
        lam_init = 0.8 - 0.6 * math.exp(-0.3 * l)
        mod8 = mod_all[l]
        lw = _layer_weights(w_in[l], mla_q_norm_g[l], mla_kv_norm_g[l], mla_w_uq[l], mla_w_ukv[l])
        (ka, qta, vta, kb, qtb, vtb, kc, qtc, vtc, dqkv, dz, dg) = _in_call(
            xu, mod8, norm1_g[l][None, :], lw, tabs, nct)

        sink_b = jnp.broadcast_to(jnp.pad(swa_sinks[l] * LOG2E, (0, 8 - SWA_HEADS))[:, None], (8, TILE))
        ya = _swa_call(qta, ka, vta, sink_b)
        g_b = jnp.broadcast_to(diff_norm_g[l][:, None], (DIFF_V_DIM, TILE))
        yb = _diff_call(qtb, kb, vtb, diff_lq1[l][None, :], diff_lk1[l][None, :], diff_lq2[l][None, :],
                        diff_lk2[l][None, :], g_b, lam_init, nct)
        yc = _mla_call(qtc, kc, vtc, nct)

        conv_w8 = jnp.pad(gdn_conv_w[l], ((0, 8 - GDN_CONV), (0, 0)))
        alog128 = jnp.pad(jnp.concatenate([gdn_a_log_f[l], gdn_a_log_b[l]]), (0, 128 - 2 * GDN_HEADS))[None, :]
        dtb128 = jnp.pad(jnp.concatenate([gdn_dt_bias_f[l], gdn_dt_bias_b[l]]), (0, 128 - 2 * GDN_HEADS))[None, :]
        gq, gk, gv, gb = _gdn_prep_call(dqkv, dg, conv_w8, alog128, dtb128, nct)
        of, ob = _gdn_scan_call(gq, gk, gv, gb, nct)

        gd = jnp.tile(gdn_norm_g[l], GDN_HEADS)[None, :]
        x1, h2, wt = _out_call(ya, yb, yc, of, ob, dz, xu, mod8, gd, w_out[l].astype(BF16), norm2_g[l][None, :],
                               moe_w_router[l].T, moe_bias[l][:, None], nct)
        xu = _moe_call(x1, h2, wt, mod8, moe_w_gate[l].astype(BF16), moe_w_up[l].astype(BF16),
                       moe_w_down[l].astype(BF16), shared_w_gate[l].astype(BF16), shared_w_up[l].astype(BF16),
                       shared_w_down[l].astype(BF16), final_norm_g[None, :], last, C)
    return xu[:, C:, :]
```

```python
import functools
import math

import jax
import jax.numpy as jnp
from jax import lax
from jax.experimental import pallas as pl
from jax.experimental.pallas import tpu as pltpu

F32 = jnp.float32
BF16 = jnp.bfloat16
HIGHEST = lax.Precision.HIGHEST

GRID_W = 64
EPS = 1e-6
ROPE_BASE = 10000.0

HEAD_DIM = 64
SWA_HEADS = 4
SWA_KV_HEADS = 2
WINDOW = 128
DIFF_HEADS = 4
DIFF_QK_DIM = 32
DIFF_V_DIM = 64
MLA_HEADS = 4
MLA_Q_RANK = 192
MLA_KV_RANK = 128
MLA_NOPE = 64
MLA_ROPE = 32
MLA_V = 64
GDN_HEADS = 4
GDN_DK = 64
GDN_DV = 64
GDN_CONV = 5
GDN_CHUNK = 64
GDN_QKV = 2 * GDN_HEADS * GDN_DK + GDN_HEADS * GDN_DV
N_EXPERTS = 64
N_GROUPS = 8
TOPK_GROUPS = 4
TOP_K = 6
EXPERT_FF = 256
ROUTED_SCALE = 2.5

IN_SPLITS = (
    SWA_HEADS * HEAD_DIM, SWA_KV_HEADS * HEAD_DIM, SWA_KV_HEADS * HEAD_DIM,
    DIFF_HEADS * 2 * DIFF_QK_DIM, DIFF_HEADS * 2 * DIFF_QK_DIM, DIFF_HEADS * DIFF_V_DIM,
    MLA_Q_RANK, MLA_KV_RANK, MLA_ROPE,
    GDN_QKV, GDN_HEADS * GDN_DV, GDN_HEADS, GDN_HEADS, GDN_HEADS, GDN_HEADS,
)

TILE = 256
LANES = 128
NEG = -1e30
LOG2E = math.log2(math.e)
VMEM_LIMIT = 56 * 1024 * 1024

_AK, _AKS, _BK, _BKS, _CQ, _CKV, _KR, _KRS, _DQKV, _DZ, _DG, _NTOK = (
    0, 128, 256, 512, 768, 1024, 1152, 1280, 1408, 2176, 2432, 2560)
_AQ, _AQS, _AV, _BQ, _BQS, _BV, _NTRN = 0, 256, 512, 640, 896, 1152, 1408


def _cparams(sem):
    return pltpu.CompilerParams(dimension_semantics=sem, vmem_limit_bytes=VMEM_LIMIT)


def _nt_dot(a, b, precision=None):
    return lax.dot_general(a, b, (((1,), (1,)), ((), ())), preferred_element_type=F32, precision=precision)


def _tn_dot(a, b):
    return lax.dot_general(a, b, (((0,), (0,)), ((), ())), preferred_element_type=F32)


def _dot(a, b, precision=None):
    return jnp.dot(a, b, preferred_element_type=F32, precision=precision)


def _split_dot(x, m):
    hi = x.astype(BF16)
    lo = (x - hi.astype(F32)).astype(BF16)
    return _dot(hi, m) + _dot(lo, m)


def _msplit_dot(m, x):
    x1 = x.astype(BF16)
    r1 = x - x1.astype(F32)
    x2 = r1.astype(BF16)
    x3 = (r1 - x2.astype(F32)).astype(BF16)
    return _dot(m, x1) + _dot(m, x2) + _dot(m, x3)


def _dot3(a, b):
    ah = a.astype(BF16)
    al = (a - ah.astype(F32)).astype(BF16)
    bh = b.astype(BF16)
    bl = (b - bh.astype(F32)).astype(BF16)
    return _dot(ah, bh) + _dot(ah, bl) + _dot(al, bh)


def _silu(x):
    return x * jax.nn.sigmoid(x)


def _group_ones(n, group):
    shift = group.bit_length() - 1
    r = lax.shift_right_logical(lax.broadcasted_iota(jnp.int32, (n, n), 0), shift)
    c = lax.shift_right_logical(lax.broadcasted_iota(jnp.int32, (n, n), 1), shift)
    return jnp.where(r == c, 1.0, 0.0).astype(BF16)


def _ada_kernel(s_ref, w_ref, b_ref, o_ref):
    s = _silu(s_ref[...])
    o_ref[0] = _dot(s, w_ref[0], precision=HIGHEST) + b_ref[0]


def _ada_call(s8, w_ada, b_ada):
    L, D, D6 = w_ada.shape
    nj = D6 // D
    return pl.pallas_call(
        _ada_kernel,
        out_shape=jax.ShapeDtypeStruct((L, 8, D6), F32),
        grid=(L, nj),
        in_specs=[
            pl.BlockSpec((8, D), lambda l, j: (0, 0)),
            pl.BlockSpec((1, D, D), lambda l, j: (l, 0, j)),
            pl.BlockSpec((1, 1, D), lambda l, j: (l, 0, j)),
        ],
        out_specs=pl.BlockSpec((1, 8, D), lambda l, j: (l, 0, j)),
        compiler_params=_cparams(("arbitrary", "arbitrary")),
        name="ada_mod",
    )(s8, w_ada, b_ada.reshape(L, 1, D6))


def _in_kernel(scales, x_ref, mod_ref, g1_ref, wtok_ref, wtrn_ref, ctok_ref, stok_ref, ctrn_ref, strn_ref,
               gq_ref, gkv_ref, wuq_ref, wuqs_ref, wkn_ref, epl_ref, wvt_ref,
               ka_ref, qta_ref, vta_ref, kb_ref, qtb_ref, vtb_ref, kc_ref, qtc_ref, vtc_ref,
               dqkv_ref, dz_ref, dg_ref):
    sc_a, sc_b, sc_c = scales
    x = x_ref[0]
    mod = mod_ref[0]
    h = x * lax.rsqrt(jnp.mean(x * x, axis=-1, keepdims=True) + EPS) * g1_ref[...]
    h = h * (1.0 + mod[1:2]) + mod[0:1]
    hb = h.astype(BF16)

    def tok(a, b):
        return _dot(hb, wtok_ref[:, a:b])

    def trn(a, b):
        return _nt_dot(wtrn_ref[a:b, :], hb)

    ct = ctok_ref[...]
    st = stok_ref[...]
    ka_ref[0] = (tok(_AK, _AKS) * ct[:, 0:128] + tok(_AKS, _BK) * st[:, 0:128]).astype(BF16)
    qta_ref[0, 0] = ((trn(_AQ, _AQS) * ctrn_ref[0, 0:256] + trn(_AQS, _AV) * strn_ref[0, 0:256]) * sc_a).astype(BF16)
    vta_ref[0, 0] = trn(_AV, _BQ).astype(BF16)
    kb_ref[0] = (tok(_BK, _BKS) * ct[:, 128:384] + tok(_BKS, _CQ) * st[:, 128:384]).astype(BF16)
    qtb_ref[0, 0] = ((trn(_BQ, _BQS) * ctrn_ref[0, 256:512] + trn(_BQS, _BV) * strn_ref[0, 256:512]) * sc_b).astype(BF16)
    vtb_ref[0, 0] = trn(_BV, _NTRN).astype(BF16)
    cq = tok(_CQ, _CKV)
    cqn = (cq * lax.rsqrt(jnp.sum(cq * cq, axis=-1, keepdims=True) * (1.0 / MLA_Q_RANK) + EPS) * gq_ref[...]).astype(BF16)
    qtc = _nt_dot(wuq_ref[...], cqn) * ctrn_ref[0, 512:1024] + _nt_dot(wuqs_ref[...], cqn) * strn_ref[0, 512:1024]
    qtc_ref[0, 0] = (qtc * sc_c).astype(BF16)
    ckv = tok(_CKV, _KR)
    ckvn = (ckv * lax.rsqrt(jnp.mean(ckv * ckv, axis=-1, keepdims=True) + EPS) * gkv_ref[...]).astype(BF16)
    krr = (tok(_KR, _KRS) * ct[:, 384:512] + tok(_KRS, _DQKV) * st[:, 384:512]).astype(BF16)
    kc_ref[0] = (_dot(ckvn, wkn_ref[...]) + _dot(krr, epl_ref[...])).astype(BF16)
    vtc_ref[0, 0] = _nt_dot(wvt_ref[...], ckvn).astype(BF16)
    dqkv_ref[0] = tok(_DQKV, _DZ)
    dz_ref[0] = tok(_DZ, _DG)
    dg_ref[0] = tok(_DG, _NTOK)


def _in_call(xu, mod8, g1, lw, tabs, nct):
    B, T, D = xu.shape
    nt = T // TILE
    tm = TILE

    def full(a):
        nd = a.ndim
        return pl.BlockSpec(a.shape, lambda b, t: (0,) * nd)

    def tokspec(w):
        return pl.BlockSpec((1, tm, w), lambda b, t: (b, t, 0))

    def trnspec(r):
        return pl.BlockSpec((1, 1, r, tm), lambda b, t: (b, t, 0, 0))

    scales = (HEAD_DIM ** -0.5 * LOG2E, DIFF_QK_DIM ** -0.5 * LOG2E, (MLA_NOPE + MLA_ROPE) ** -0.5 * LOG2E)
    weights = (lw["wtok"], lw["wtrn"])
    tables = (tabs["ctok"], tabs["stok"], tabs["ctrn"], tabs["strn"])
    mla = (lw["gq"], lw["gkv"], lw["wuq"], lw["wuqs"], lw["wkn"], lw["epl"], lw["wvt"])
    in_specs = [
        tokspec(D),
        pl.BlockSpec((1, 6, D), lambda b, t: (jnp.where(t < nct, 0, b + 1), 0, 0)),
        full(g1), full(weights[0]), full(weights[1]),
        pl.BlockSpec((tm, 512), lambda b, t: (t, 0)),
        pl.BlockSpec((tm, 512), lambda b, t: (t, 0)),
        pl.BlockSpec((1, 1024, tm), lambda b, t: (t, 0, 0)),
        pl.BlockSpec((1, 1024, tm), lambda b, t: (t, 0, 0)),
    ] + [full(a) for a in mla]
    out_shape = (
        jax.ShapeDtypeStruct((B, T, 128), BF16), jax.ShapeDtypeStruct((B, nt, 256, tm), BF16),
        jax.ShapeDtypeStruct((B, nt, 128, tm), BF16),
        jax.ShapeDtypeStruct((B, T, 256), BF16), jax.ShapeDtypeStruct((B, nt, 256, tm), BF16),
        jax.ShapeDtypeStruct((B, nt, 256, tm), BF16),
        jax.ShapeDtypeStruct((B, T, 512), BF16), jax.ShapeDtypeStruct((B, nt, 512, tm), BF16),
        jax.ShapeDtypeStruct((B, nt, 256, tm), BF16),
        jax.ShapeDtypeStruct((B, T, GDN_QKV), F32), jax.ShapeDtypeStruct((B, T, 256), F32),
        jax.ShapeDtypeStruct((B, T, 128), F32),
    )
    out_specs = (
        tokspec(128), trnspec(256), trnspec(128),
        tokspec(256), trnspec(256), trnspec(256),
        tokspec(512), trnspec(512), trnspec(256),
        tokspec(GDN_QKV), tokspec(256), tokspec(128),
    )
    return pl.pallas_call(
        functools.partial(_in_kernel, scales),
        out_shape=out_shape, grid=(B, nt), in_specs=in_specs, out_specs=out_specs,
        compiler_params=_cparams(("arbitrary", "arbitrary")),
        name="in_proj",
    )(xu, mod8, g1, *weights, *tables, *mla)


def _swa_kernel(nt, q_ref, k0_ref, k1_ref, k2_ref, k3_ref, v0_ref, v1_ref, v2_ref, v3_ref, sink_ref,
                o_ref, yt_ref):
    t = pl.program_id(1)
    tq = TILE
    r = lax.broadcasted_iota(jnp.int32, (tq, tq), 0)
    q = lax.broadcasted_iota(jnp.int32, (tq, tq), 1)
    d = r - q
    zero = jnp.zeros((tq, tq), F32)
    neg = jnp.full((tq, tq), NEG, F32)

    def gate(cond):
        return jnp.where(cond, 0.0, NEG).astype(F32)

    bias = (
        None,
        jnp.where(d >= WINDOW, zero, neg) + gate(t >= 2),
        jnp.where(jnp.abs(d) <= WINDOW, zero, neg) + gate(t >= 1),
        jnp.where(d <= -WINDOW, zero, neg) + gate(jnp.logical_and(t >= 1, t <= nt - 2)),
    )
    krefs = (k0_ref, k1_ref, k2_ref, k3_ref)
    vrefs = (v0_ref, v1_ref, v2_ref, v3_ref)
    zq = jnp.zeros((HEAD_DIM, tq), BF16)
    group = SWA_HEADS // SWA_KV_HEADS
    for h in range(SWA_HEADS):
        g = h // group
        qh = q_ref[0, 0, HEAD_DIM * h:HEAD_DIM * (h + 1), :]
        wq = jnp.concatenate([qh, zq], axis=0) if g == 0 else jnp.concatenate([zq, qh], axis=0)
        sink = sink_ref[h:h + 1, :]
        s = []
        m = sink
        for j in range(4):
            sj = _dot(krefs[j][0], wq)
            if bias[j] is not None:
                sj = sj + bias[j]
            s.append(sj)
            m = jnp.maximum(m, jnp.max(sj, axis=0, keepdims=True))
        l = jnp.exp2(sink - m)
        acc = jnp.zeros((HEAD_DIM, tq), F32)
        for j in range(4):
            p = jnp.exp2(s[j] - m)
            l = l + jnp.sum(p, axis=0, keepdims=True)
            acc = acc + _dot(vrefs[j][0, 0, HEAD_DIM * g:HEAD_DIM * (g + 1), :], p.astype(BF16))
        yt_ref[HEAD_DIM * h:HEAD_DIM * (h + 1), :] = acc / l
    o_ref[0] = yt_ref[...].T.astype(BF16)


def _swa_call(qta, ka, vta, sink_b):
    B, nt, R, tm = qta.shape
    T = nt * tm

    def kspec(f):
        return pl.BlockSpec((1, tm, 128), lambda b, t: (b, f(t), 0))

    def vspec(f):
        return pl.BlockSpec((1, 1, 128, tm), lambda b, t: (b, f(t), 0, 0))

    fs = (lambda t: 0, lambda t: jnp.maximum(t - 1, 0), lambda t: t, lambda t: jnp.minimum(t + 1, nt - 1))
    return pl.pallas_call(
        functools.partial(_swa_kernel, nt),
        out_shape=jax.ShapeDtypeStruct((B, T, 256), BF16),
        grid=(B, nt),
        in_specs=[pl.BlockSpec((1, 1, R, tm), lambda b, t: (b, t, 0, 0))]
        + [kspec(f) for f in fs] + [vspec(f) for f in fs]
        + [pl.BlockSpec(sink_b.shape, lambda b, t: (0, 0))],
        out_specs=pl.BlockSpec((1, tm, 256), lambda b, t: (b, t, 0)),
        scratch_shapes=[pltpu.VMEM((256, tm), F32)],
        compiler_params=_cparams(("arbitrary", "arbitrary")),
        name="swa_attn",
    )(qta, ka, ka, ka, ka, vta, vta, vta, vta, sink_b)


def _flash_maps(n_maps, n_pairs, get_k, get_wq, get_v, s_ref, p_ref, a_ref, l_ref, acc_ref):
    tq = TILE
    n_last = 2 * n_pairs

    def stage_scores(c, slot):
        cc = jnp.minimum(c, n_last)
        for j in range(n_maps):
            s_ref[slot, j] = _dot(get_k(cc, j), get_wq(j))

    def stage_softmax(slot, ms, ls):
        new_m, new_l = [], []
        for j in range(n_maps):
            s = s_ref[slot, j]
            m_new = jnp.maximum(ms[j], jnp.max(s, axis=0, keepdims=True))
            alpha = jnp.exp2(ms[j] - m_new)
            p = jnp.exp2(s - m_new)
            p_ref[slot, j] = p.astype(BF16)
            a_ref[slot, j, 0:1, :] = alpha
            new_l.append(alpha * ls[j] + jnp.sum(p, axis=0, keepdims=True))
            new_m.append(m_new)
        return tuple(new_m), tuple(new_l)

    def stage_values(c, slot):
        for j in range(n_maps):
            acc_ref[j] = a_ref[slot, j, 0:1, :] * acc_ref[j] + _dot(get_v(c, j), p_ref[slot, j])

    acc_ref[...] = jnp.zeros(acc_ref.shape, F32)
    init = (tuple(jnp.full((1, tq), NEG, F32) for _ in range(n_maps)),
            tuple(jnp.zeros((1, tq), F32) for _ in range(n_maps)))
    stage_scores(0, 0)
    stage_scores(1, 1)
    carry = stage_softmax(0, *init)

    def body(i, carry):
        c = 2 * i + 1
        stage_scores(c + 1, 0)
        carry = stage_softmax(1, *carry)
        stage_values(c - 1, 0)
        stage_scores(c + 2, 1)
        carry = stage_softmax(0, *carry)
        stage_values(c, 1)
        return carry

    _, ls = lax.fori_loop(0, n_pairs, body, carry)
    stage_values(n_last, 0)
    for j in range(n_maps):
        l_ref[j:j + 1, :] = ls[j]


def _diff_kernel(nct, nt, lam_init, q_ref, k_ref, v_ref, lq1_ref, lk1_ref, lq2_ref, lk2_ref, g_ref,
                 o_ref, qm_ref, s_ref, p_ref, a_ref, l_ref, acc_ref, yt_ref):
    t = pl.program_id(1)
    tq = TILE
    n_maps = 2 * DIFF_HEADS
    rows = lax.broadcasted_iota(jnp.int32, (LANES, tq), 0)
    per = LANES // DIFF_QK_DIM
    for j in range(n_maps):
        grp, sub = j // per, j % per
        qg = q_ref[0, 0, LANES * grp:LANES * (grp + 1), :]
        keep = jnp.logical_and(rows >= DIFF_QK_DIM * sub, rows < DIFF_QK_DIM * (sub + 1))
        qm_ref[j] = jnp.where(keep, qg, jnp.zeros_like(qg))

    def get_k(c, j):
        grp = j // per
        return k_ref[0, pl.ds(pl.multiple_of(c * tq, tq), tq), LANES * grp:LANES * (grp + 1)]

    def get_wq(j):
        return qm_ref[j]

    def get_v(c, j):
        h = j // 2
        return v_ref[0, c, DIFF_V_DIM * h:DIFF_V_DIM * (h + 1), :]

    n_pairs = jnp.where(t < nct, 0, (nt - nct) // 2)
    _flash_maps(n_maps, n_pairs, get_k, get_wq, get_v, s_ref, p_ref, a_ref, l_ref, acc_ref)

    lam =(jnp.exp(jnp.sum(lq1_ref[...] * lk1_ref[...], axis=-1, keepdims=True))
           - jnp.exp(jnp.sum(lq2_ref[...] * lk2_ref[...], axis=-1, keepdims=True)) + lam_init)
    for h in range(DIFF_HEADS):
        o1 = acc_ref[2 * h] / l_ref[2 * h:2 * h + 1, :]
        o2 = acc_ref[2 * h + 1] / l_ref[2 * h + 1:2 * h + 2, :]
        o = o1 - lam * o2
        on = o * lax.rsqrt(jnp.mean(o * o, axis=0, keepdims=True) + EPS) * g_ref[...]
        yt_ref[DIFF_V_DIM * h:DIFF_V_DIM * (h + 1), :] = on * (1.0 - lam_init)
    o_ref[0] = yt_ref[...].T.astype(BF16)


def _diff_call(qtb, kb, vtb, lq1, lk1, lq2, lk2, g_b, lam_init, nct):
    B, nt, R, tm = qtb.shape
    T = nt * tm
    n_maps = 2 * DIFF_HEADS
    small = [lq1, lk1, lq2, lk2, g_b]
    return pl.pallas_call(
        functools.partial(_diff_kernel, nct, nt, lam_init),
        out_shape=jax.ShapeDtypeStruct((B, T, 256), BF16),
        grid=(B, nt),
        in_specs=[
            pl.BlockSpec((1, 1, R, tm), lambda b, t: (b, t, 0, 0)),
            pl.BlockSpec((1, T, 256), lambda b, t: (b, 0, 0)),
            pl.BlockSpec((1, nt, 256, tm), lambda b, t: (b, 0, 0, 0)),
        ] + [pl.BlockSpec(a.shape, lambda b, t: (0, 0)) for a in small],
        out_specs=pl.BlockSpec((1, tm, 256), lambda b, t: (b, t, 0)),
        scratch_shapes=[
            pltpu.VMEM((n_maps, LANES, tm), BF16),
            pltpu.VMEM((2, n_maps, tm, tm), F32), pltpu.VMEM((2, n_maps, tm, tm), BF16),
            pltpu.VMEM((2, n_maps, 8, tm), F32), pltpu.VMEM((n_maps, tm), F32),
            pltpu.VMEM((n_maps, DIFF_V_DIM, tm), F32),
            pltpu.VMEM((256, tm), F32),
        ],
        compiler_params=_cparams(("arbitrary", "arbitrary")),
        name="diff_attn",
    )(qtb, kb, vtb, *small)


def _mla_kernel(nct, nt, q_ref, k_ref, v_ref, o_ref, s_ref, p_ref, a_ref, l_ref, acc_ref, yt_ref):
    t = pl.program_id(1)
    tq = TILE

    def get_k(c, j):
        return k_ref[0, pl.ds(pl.multiple_of(c * tq, tq), tq), LANES * j:LANES * (j + 1)]

    def get_wq(j):
        return q_ref[0, 0, LANES * j:LANES * (j + 1), :]

    def get_v(c, j):
        return v_ref[0, c, MLA_V * j:MLA_V * (j + 1), :]

    n_pairs = jnp.where(t < nct, 0, (nt - nct) // 2)
    _flash_maps(MLA_HEADS, n_pairs, get_k, get_wq, get_v, s_ref, p_ref, a_ref, l_ref, acc_ref)
    for h in range(MLA_HEADS):
        yt_ref[MLA_V * h:MLA_V * (h + 1), :] = acc_ref[h] / l_ref[h:h + 1, :]
    o_ref[0] = yt_ref[...].T.astype(BF16)


def _mla_call(qtc, kc, vtc, nct):
    B, nt, R, tm = qtc.shape
    T = nt * tm
    return pl.pallas_call(
        functools.partial(_mla_kernel, nct, nt),
        out_shape=jax.ShapeDtypeStruct((B, T, 256), BF16),
        grid=(B, nt),
        in_specs=[
            pl.BlockSpec((1, 1, R, tm), lambda b, t: (b, t, 0, 0)),
            pl.BlockSpec((1, T, 512), lambda b, t: (b, 0, 0)),
            pl.BlockSpec((1, nt, 256, tm), lambda b, t: (b, 0, 0, 0)),
        ],
        out_specs=pl.BlockSpec((1, tm, 256), lambda b, t: (b, t, 0)),
        scratch_shapes=[
            pltpu.VMEM((2, MLA_HEADS, tm, tm), F32), pltpu.VMEM((2, MLA_HEADS, tm, tm), BF16),
            pltpu.VMEM((2, MLA_HEADS, 8, tm), F32), pltpu.VMEM((8, tm), F32),
            pltpu.VMEM((MLA_HEADS, MLA_V, tm), F32),
            pltpu.VMEM((256, tm), F32),
        ],
        compiler_params=_cparams(("arbitrary", "arbitrary")),
        name="mla_attn",
    )(qtc, kc, vtc)


def _gdn_prep_kernel(nct, nt, x_ref, xp_ref, xn_ref, w_ref, alog_ref, dtb_ref, dg_ref,
                     q_ref, k_ref, v_ref, gb_ref, xe_ref):
    t = pl.program_id(1)
    tm = TILE
    pad = GDN_CONV // 2
    keep_prev = jnp.where(jnp.logical_or(t == 0, t == nct), 0.0, 1.0).astype(F32)
    keep_next = jnp.where(jnp.logical_or(t == nct - 1, t == nt - 1), 0.0, 1.0).astype(F32)
    xe_ref[0:8, :] = xp_ref[0] * keep_prev
    xe_ref[8:8 + tm, :] = x_ref[0]
    xe_ref[8 + tm:16 + tm, :] = xn_ref[0] * keep_next
    y = jnp.zeros((tm, GDN_QKV), F32)
    for j in range(GDN_CONV):
        y = y + w_ref[j:j + 1, :] * xe_ref[8 - pad + j:8 - pad + j + tm, :]
    y = _silu(y)
    hq = GDN_HEADS * GDN_DK
    ones = _group_ones(hq, GDN_DK)
    q = y[:, 0:hq]
    k = y[:, hq:2 * hq]
    q_ref[0] = q * lax.rsqrt(_split_dot(q * q, ones) + EPS) * (GDN_DK ** -0.5)
    k_ref[0] = k * lax.rsqrt(_split_dot(k * k, ones) + EPS)
    v_ref[0] = y[:, 2 * hq:]
    dg = dg_ref[0]
    lane = lax.broadcasted_iota(jnp.int32, dg.shape, 1)
    xs = dg + dtb_ref[...]
    softplus = jnp.maximum(xs, 0.0) + jnp.log1p(jnp.exp(-jnp.abs(xs)))
    gate = -jnp.exp(alog_ref[...]) * softplus
    gb_ref[0] = jnp.where(lane < 2 * GDN_HEADS, gate, jax.nn.sigmoid(dg))


def _gdn_prep_call(dqkv, dg, conv_w8, alog128, dtb128, nct):
    B, T, W = dqkv.shape
    nt = T // TILE
    tm = TILE
    hb = tm // 8
    return pl.pallas_call(
        functools.partial(_gdn_prep_kernel, nct, nt),
        out_shape=(jax.ShapeDtypeStruct((B, T, 256), F32),) * 3 + (jax.ShapeDtypeStruct((B, T, 128), F32),),
        grid=(B, nt),
        in_specs=[
            pl.BlockSpec((1, tm, W), lambda b, t: (b, t, 0)),
            pl.BlockSpec((1, 8, W), lambda b, t: (b, jnp.maximum(t * hb - 1, 0), 0)),
            pl.BlockSpec((1, 8, W), lambda b, t: (b, jnp.minimum((t + 1) * hb, T // 8 - 1), 0)),
            pl.BlockSpec((8, W), lambda b, t: (0, 0)),
            pl.BlockSpec((1, 128), lambda b, t: (0, 0)),
            pl.BlockSpec((1, 128), lambda b, t: (0, 0)),
            pl.BlockSpec((1, tm, 128), lambda b, t: (b, t, 0)),
        ],
        out_specs=(pl.BlockSpec((1, tm, 256), lambda b, t: (b, t, 0)),) * 3
        + (pl.BlockSpec((1, tm, 128), lambda b, t: (b, t, 0)),),
        scratch_shapes=[pltpu.VMEM((tm + 16, W), F32)],
        compiler_params=_cparams(("arbitrary", "arbitrary")),
        name="gdn_prep",
    )(dqkv, dqkv, dqkv, conv_w8, alog128, dtb128, dg)


def _gdn_chain(q, k, v, g_col, beta_col, s_old, incl, strict, last_row):
    cs = GDN_CHUNK
    inclb = incl.astype(BF16)
    gbc = jnp.broadcast_to(g_col, (cs, cs))
    gc = _msplit_dot(inclb, gbc)
    gdiff = _msplit_dot(inclb, gbc * strict)
    decay = jnp.exp(gdiff * incl) * incl
    eg = jnp.exp(gc)
    gl = gc[last_row:last_row + 1, :]
    beta = jnp.broadcast_to(beta_col, (cs, cs))
    kb = k * beta
    vb = v * beta
    kbf = k.astype(BF16)
    a = _nt_dot(kb.astype(BF16), kbf) * decay * strict
    eye = incl - strict
    tm_ = eye - a
    pw = a
    for _ in range(5):
        pw = _dot3(pw, pw)
        tm_ = tm_ + _dot3(tm_, pw)
    tb = tm_.astype(BF16)
    u = _dot(tb, vb.astype(BF16))
    w = _dot(tb, (kb * eg).astype(BF16))
    attn = _nt_dot(q.astype(BF16), kbf) * decay
    qg = q * eg
    kdec = k * jnp.exp(gl - gc)
    sb = s_old.astype(BF16)
    v_new = u - _dot(w.astype(BF16), sb)
    o = _dot(qg.astype(BF16), sb) + _dot(attn.astype(BF16), v_new.astype(BF16))
    s_new = s_old * jnp.exp(gl) + _tn_dot(kdec.astype(BF16), v_new.astype(BF16))
    return o, s_new


def _gdn_scan_kernel(qf_ref, kf_ref, vf_ref, gf_ref, qb_ref, kb_ref, vb_ref, gbk_ref,
                     of_ref, ob_ref, sf_ref, sb_ref):
    s = pl.program_id(1)
    cs = GDN_CHUNK

    @pl.when(s == 0)
    def _():
        sf_ref[...] = jnp.zeros(sf_ref.shape, F32)
        sb_ref[...] = jnp.zeros(sb_ref.shape, F32)

    r = lax.broadcasted_iota(jnp.int32, (cs, cs), 0)
    c = lax.broadcasted_iota(jnp.int32, (cs, cs), 1)
    one = jnp.ones((cs, cs), F32)
    zero = jnp.zeros((cs, cs), F32)
    dirs = (
        (qf_ref, kf_ref, vf_ref, gf_ref, of_ref, sf_ref, jnp.where(r >= c, one, zero), jnp.where(r > c, one, zero), cs - 1, 0),
        (qb_ref, kb_ref, vb_ref, gbk_ref, ob_ref, sb_ref, jnp.where(r <= c, one, zero), jnp.where(r < c, one, zero), 0, GDN_HEADS),
    )
    for q_ref, k_ref, v_ref, g_ref, o_ref, st_ref, incl, strict, last_row, goff in dirs:
        gates = g_ref[0]
        outs = []
        for h in range(GDN_HEADS):
            sl = slice(GDN_DK * h, GDN_DK * (h + 1))
            g_col = gates[:, goff + h:goff + h + 1]
            beta_col = gates[:, 2 * GDN_HEADS + goff + h:2 * GDN_HEADS + goff + h + 1]
            o, s_new = _gdn_chain(q_ref[0][:, sl], k_ref[0][:, sl], v_ref[0][:, sl], g_col, beta_col,
                                  st_ref[h], incl, strict, last_row)
            st_ref[h] = s_new
            outs.append(o)
        o_ref[0] = jnp.concatenate(outs, axis=-1)


def _gdn_scan_call(q, k, v, gb, nct):
    B, T, W = q.shape
    cs = GDN_CHUNK
    nc = T // cs
    ncc = nct * (TILE // cs)

    def fwd(b, s):
        return (b, s, 0)

    def bwd(b, s):
        return (b, jnp.where(s < ncc, ncc - 1 - s, nc - 1 - (s - ncc)), 0)

    specs = []
    for f in (fwd, bwd):
        specs += [pl.BlockSpec((1, cs, W), f)] * 3 + [pl.BlockSpec((1, cs, 128), f)]
    return pl.pallas_call(
        _gdn_scan_kernel,
        out_shape=(jax.ShapeDtypeStruct((B, T, W), F32),) * 2,
        grid=(B, nc),
        in_specs=specs,
        out_specs=(pl.BlockSpec((1, cs, W), fwd), pl.BlockSpec((1, cs, W), bwd)),
        scratch_shapes=[pltpu.VMEM((GDN_HEADS, GDN_DK, GDN_DV), F32)] * 2,
        compiler_params=_cparams(("arbitrary", "arbitrary")),
        name="gdn_scan",
    )(q, k, v, gb, q, k, v, gb)


def _route(logits_t, bias_col):
    ne, tm = logits_t.shape
    per = ne // N_GROUPS
    neg_inf = -jnp.inf
    scores = jax.nn.sigmoid(logits_t)
    choice = scores + bias_col
    idx_m = lax.broadcasted_iota(jnp.int32, (per, tm), 0).astype(F32)
    groups, gscore = [], []
    for g in range(N_GROUPS):
        cg = choice[per * g:per * (g + 1), :]
        m1 = jnp.max(cg, axis=0, keepdims=True)
        i1 = jnp.min(jnp.where(cg == m1, idx_m, float(per)), axis=0, keepdims=True)
        m2 = jnp.max(jnp.where(idx_m == i1, neg_inf, cg), axis=0, keepdims=True)
        groups.append(cg)
        gscore.append(m1 + m2)
    gsel = [jnp.zeros((1, tm), F32) for _ in range(N_GROUPS)]
    for _ in range(TOPK_GROUPS):
        gm = functools.reduce(jnp.maximum, gscore)
        found = jnp.zeros((1, tm), F32)
        for g in range(N_GROUPS):
            hit = jnp.logical_and(gscore[g] == gm, found < 0.5)
            gsel[g] = jnp.where(hit, 1.0, gsel[g])
            gscore[g] = jnp.where(hit, neg_inf, gscore[g])
            found = jnp.where(hit, 1.0, found)
    mc = jnp.concatenate([jnp.where(gsel[g] > 0.5, groups[g], neg_inf) for g in range(N_GROUPS)], axis=0)
    idx_e = lax.broadcasted_iota(jnp.int32, (ne, tm), 0).astype(F32)
    esel = jnp.zeros((ne, tm), F32)
    for _ in range(TOP_K):
        em = jnp.max(mc, axis=0, keepdims=True)
        ei = jnp.min(jnp.where(mc == em, idx_e, float(ne)), axis=0, keepdims=True)
        hit = idx_e == ei
        esel = jnp.where(hit, 1.0, esel)
        mc = jnp.where(hit, neg_inf, mc)
    picked = scores * esel
    return picked / jnp.sum(picked, axis=0, keepdims=True) * ROUTED_SCALE


def _out_kernel(ya_ref, yb_ref, yc_ref, of_ref, ob_ref, z_ref, x_ref, mod_ref, gd_ref, wo_ref, g2_ref,
                wr_ref, rb_ref, x1_ref, h2_ref, wt_ref):
    o = of_ref[0] + ob_ref[0]
    ones = _group_ones(GDN_HEADS * GDN_DV, GDN_DV)
    ms = _split_dot(o * o, ones) * (1.0 / GDN_DV)
    yd = o * lax.rsqrt(ms + EPS) * gd_ref[...] * _silu(z_ref[0])
    y = (_dot(ya_ref[0], wo_ref[0:256, :]) + _dot(yb_ref[0], wo_ref[256:512, :])
         + _dot(yc_ref[0], wo_ref[512:768, :]) + _dot(yd.astype(BF16), wo_ref[768:1024, :]))
    mod = mod_ref[0]
    x1 = x_ref[0] + mod[2:3] * y
    x1_ref[0] = x1
    h2 = x1 * lax.rsqrt(jnp.mean(x1 * x1, axis=-1, keepdims=True) + EPS) * g2_ref[...]
    h2 = h2 * (1.0 + mod[4:5]) + mod[3:4]
    h2_ref[0] = h2.astype(BF16)
    logits_t = _nt_dot(wr_ref[...], h2, precision=HIGHEST)
    w_t = _route(logits_t, rb_ref[...])
    w_pad = jnp.concatenate([w_t, jnp.zeros((LANES - N_EXPERTS, w_t.shape[1]), F32)], axis=0)
    wt_ref[0] = w_pad.T


def _out_call(ya, yb, yc, of, ob, dz, xu, mod8, gd, wo, g2, wr_t, rbias, nct):
    B, T, D = xu.shape
    nt = T // TILE
    tm = TILE

    def tokspec(w):
        return pl.BlockSpec((1, tm, w), lambda b, t: (b, t, 0))

    def full(a):
        nd = a.ndim
        return pl.BlockSpec(a.shape, lambda b, t: (0,) * nd)

    return pl.pallas_call(
        _out_kernel,
        out_shape=(jax.ShapeDtypeStruct((B, T, D), F32), jax.ShapeDtypeStruct((B, T, D), BF16),
                   jax.ShapeDtypeStruct((B, T, 128), F32)),
        grid=(B, nt),
        in_specs=[tokspec(256)] * 6 + [
            tokspec(D),
            pl.BlockSpec((1, 6, D), lambda b, t: (jnp.where(t < nct, 0, b + 1), 0, 0)),
            full(gd), full(wo), full(g2), full(wr_t), full(rbias),
        ],
        out_specs=(tokspec(D), tokspec(D), tokspec(128)),
        compiler_params=_cparams(("arbitrary", "arbitrary")),
        name="out_proj_route",
    )(ya, yb, yc, of, ob, dz, xu, mod8, gd, wo, g2, wr_t, rbias)


def _moe_kernel(last, n_ctx_rows, x1_ref, h_ref, wt_ref, modc_ref, modl_ref, wg_ref, wu_ref, wd_ref,
                sg_ref, su_ref, sd_ref, gf_ref, o_ref, acc_ref):
    i = pl.program_id(1)
    e = pl.program_id(2)
    tm = h_ref.shape[1]

    @pl.when(e == 0)
    def _():
        acc_ref[...] = jnp.zeros(acc_ref.shape, F32)

    @pl.when(e < N_EXPERTS)
    def _():
        x = h_ref[0]
        wt = wt_ref[0]
        lane = lax.broadcasted_iota(jnp.int32, wt.shape, 1)
        col = jnp.sum(jnp.where(lane == e, wt, 0.0), axis=-1, keepdims=True)
        hh = _silu(_dot(x, wg_ref[0])) * _dot(x, wu_ref[0]) * col
        acc_ref[...] += _dot(hh.astype(BF16), wd_ref[0])

    @pl.when(e == N_EXPERTS)
    def _():
        x = h_ref[0]
        hh = _silu(_dot(x, sg_ref[...])) * _dot(x, su_ref[...])
        f = acc_ref[...] + _dot(hh.astype(BF16), sd_ref[...])
        row = lax.broadcasted_iota(jnp.int32, (tm, 1), 0) + i * tm
        gate = jnp.where(row < n_ctx_rows, modc_ref[0][5:6], modl_ref[0][5:6])
        x2 = x1_ref[0] + gate * f
        if last:
            x2 = x2 * lax.rsqrt(jnp.mean(x2 * x2, axis=-1, keepdims=True) + EPS) * gf_ref[...]
        o_ref[0] = x2


def _moe_tile(T):
    best = LANES
    for cand in range(LANES, 1408 + 1, LANES):
        if T % cand == 0:
            best = cand
    return best


def _moe_call(x1, h2, wt, mod8, wg, wu, wd, sg, su, sd, gfin, last, n_ctx_rows):
    B, T, D = x1.shape
    tm = _moe_tile(T)
    F = wg.shape[-1]

    def tokspec(w):
        return pl.BlockSpec((1, tm, w), lambda b, i, e: (b, i, 0))

    def full(a):
        nd = a.ndim
        return pl.BlockSpec(a.shape, lambda b, i, e: (0,) * nd)

    def espec(shape):
        return pl.BlockSpec((1,) + shape, lambda b, i, e: (jnp.minimum(e, N_EXPERTS - 1), 0, 0))

    return pl.pallas_call(
        functools.partial(_moe_kernel, last, n_ctx_rows),
        out_shape=jax.ShapeDtypeStruct((B, T, D), F32),
        grid=(B, T // tm, N_EXPERTS + 1),
        in_specs=[
            tokspec(D), tokspec(D), tokspec(128),
            pl.BlockSpec((1, 6, D), lambda b, i, e: (0, 0, 0)),
            pl.BlockSpec((1, 6, D), lambda b, i, e: (b + 1, 0, 0)),
            espec((D, F)), espec((D, F)), espec((F, D)),
            full(sg), full(su), full(sd), full(gfin),
        ],
        out_specs=tokspec(D),
        scratch_shapes=[pltpu.VMEM((tm, D), F32)],
        compiler_params=_cparams(("arbitrary", "arbitrary", "arbitrary")),
        name="moe_dense",
    )(x1, h2, wt, mod8, mod8, wg, wu, wd, sg, su, sd, gfin)


def _swap_cols(w, dim):
    d0, n = w.shape[0], w.shape[1] // dim
    w4 = w.reshape(d0, n, 2, dim // 2)
    return jnp.concatenate([-w4[:, :, 1:], w4[:, :, :1]], axis=2).reshape(d0, n * dim)


def _pad_cols(w, width):
    return jnp.pad(w, ((0, 0), (0, width - w.shape[1])))


def _layer_weights(w_in, gq, gkv, w_uq, w_ukv):
    cuts, acc = [], 0
    for s in IN_SPLITS[:-1]:
        acc += s
        cuts.append(acc)
    (aq, ak, av, bq, bk, bv, cq, ckv, kr, dqkv, dz, daf, dab, dbf, dbb) = jnp.split(w_in, cuts, axis=-1)
    wtok = jnp.concatenate([
        ak, _swap_cols(ak, HEAD_DIM), bk, _swap_cols(bk, DIFF_QK_DIM),
        _pad_cols(cq, 256), ckv, _pad_cols(kr, 128), _pad_cols(_swap_cols(kr, MLA_ROPE), 128),
        dqkv, dz, _pad_cols(jnp.concatenate([daf, dab, dbf, dbb], axis=-1), 128),
    ], axis=-1).astype(BF16)
    wtrn = jnp.concatenate([
        aq, _swap_cols(aq, HEAD_DIM), av, bq, _swap_cols(bq, DIFF_QK_DIM), bv,
    ], axis=-1).T.astype(BF16)
    qd = MLA_NOPE + MLA_ROPE
    uq = w_uq.reshape(MLA_Q_RANK, MLA_HEADS, qd)
    uq_n, uq_r = uq[:, :, :MLA_NOPE], uq[:, :, MLA_NOPE:]
    uq_rs = jnp.concatenate([-uq_r[:, :, MLA_ROPE // 2:], uq_r[:, :, :MLA_ROPE // 2]], axis=-1)
    zpad = jnp.zeros((MLA_Q_RANK, MLA_HEADS, LANES - qd), F32)
    wuq = jnp.concatenate([uq_n, uq_r, zpad], axis=-1).reshape(MLA_Q_RANK, MLA_HEADS * LANES)
    wuqs = jnp.concatenate([jnp.zeros_like(uq_n), uq_rs, zpad], axis=-1).reshape(MLA_Q_RANK, MLA_HEADS * LANES)
    wuq = jnp.pad(wuq.T, ((0, 0), (0, 256 - MLA_Q_RANK))).astype(BF16)
    wuqs = jnp.pad(wuqs.T, ((0, 0), (0, 256 - MLA_Q_RANK))).astype(BF16)
    ukv = w_ukv.reshape(MLA_KV_RANK, MLA_HEADS, MLA_NOPE + MLA_V)
    wkn = jnp.concatenate([ukv[:, :, :MLA_NOPE], jnp.zeros((MLA_KV_RANK, MLA_HEADS, LANES - MLA_NOPE), F32)],
                          axis=-1).reshape(MLA_KV_RANK, MLA_HEADS * LANES).astype(BF16)
    wvt = ukv[:, :, MLA_NOPE:].reshape(MLA_KV_RANK, MLA_HEADS * MLA_V).T.astype(BF16)
    rr = jnp.arange(LANES)[:, None]
    cc = jnp.arange(MLA_HEADS * LANES)[None, :]
    epl = jnp.where(jnp.logical_and(rr < MLA_ROPE, (cc % LANES) == rr + MLA_NOPE), 1.0, 0.0).astype(BF16)
    return dict(wtok=wtok, wtrn=wtrn, gq=_pad_cols(gq[None, :], 256), gkv=gkv[None, :],
                wuq=wuq, wuqs=wuqs, wkn=wkn, epl=epl, wvt=wvt)


def _rope_full(S, C, dim):
    quarter = dim // 4
    inv = ROPE_BASE ** (-jnp.arange(quarter, dtype=F32) / quarter)
    rows = S // GRID_W
    r = jnp.repeat(jnp.arange(rows, dtype=F32), GRID_W)
    col = jnp.tile(jnp.arange(GRID_W, dtype=F32), rows)
    ang = jnp.concatenate([r[:, None] * inv, col[:, None] * inv], axis=-1)
    cos = jnp.concatenate([jnp.ones((C, dim // 2), F32), jnp.cos(ang)], axis=0)
    sin = jnp.concatenate([jnp.zeros((C, dim // 2), F32), jnp.sin(ang)], axis=0)
    return jnp.concatenate([cos, cos], axis=-1), jnp.concatenate([sin, sin], axis=-1)


def _rope_tables(S, C):
    T = S + C
    nt = T // TILE
    ca, sa = _rope_full(S, C, HEAD_DIM)
    cb, sb = _rope_full(S, C, DIFF_QK_DIM)
    cr, sr = _rope_full(S, C, MLA_ROPE)
    ctok = jnp.concatenate([jnp.tile(ca, (1, 2)), jnp.tile(cb, (1, 8)), _pad_cols(cr, 128)], axis=-1)
    stok = jnp.concatenate([jnp.tile(sa, (1, 2)), jnp.tile(sb, (1, 8)), _pad_cols(sr, 128)], axis=-1)
    one = jnp.ones((T, MLA_NOPE), F32)
    zero = jnp.zeros((T, MLA_NOPE), F32)
    padz = jnp.zeros((T, LANES - MLA_NOPE - MLA_ROPE), F32)
    cmla = jnp.tile(jnp.concatenate([one, cr, padz], axis=-1), (1, MLA_HEADS))
    smla = jnp.tile(jnp.concatenate([zero, sr, padz], axis=-1), (1, MLA_HEADS))
    ctrn = jnp.concatenate([jnp.tile(ca, (1, 4)), jnp.tile(cb, (1, 8)), cmla], axis=-1)
    strn = jnp.concatenate([jnp.tile(sa, (1, 4)), jnp.tile(sb, (1, 8)), smla], axis=-1)

    def blocked_t(a):
        return a.reshape(nt, TILE, a.shape[1]).transpose(0, 2, 1)

    return dict(ctok=ctok, stok=stok, ctrn=blocked_t(ctrn), strn=blocked_t(strn))


def kernel(x, c, ctx, c_ctx, w_ada, b_ada, norm1_g, norm2_g, w_in, w_out, swa_sinks, diff_lq1, diff_lk1, diff_lq2, diff_lk2, diff_norm_g, mla_q_norm_g, mla_kv_norm_g, mla_w_uq, mla_w_ukv, gdn_conv_w, gdn_a_log_f, gdn_a_log_b, gdn_dt_bias_f, gdn_dt_bias_b, gdn_norm_g, moe_w_router, moe_bias, moe_w_gate, moe_w_up, moe_w_down, shared_w_gate, shared_w_up, shared_w_down, final_norm_g):
    B, S, D = x.shape
    C = ctx.shape[1]
    L = w_ada.shape[0]
    assert S % (2 * TILE) == 0 and C == TILE and B + 1 <= 8 and S % GRID_W == 0
    T = S + C
    nct = C // TILE

    xu = jnp.concatenate([ctx, x], axis=1)
    s8 = jnp.zeros((8, D), F32).at[0].set(c_ctx).at[1:B + 1].set(c)
    mod_all = _ada_call(s8, w_ada, b_ada).reshape(L, 8, 6, D)
    tabs = _rope_tables(S, C)

    for l in range(L):
        last = l == L - 1
        lam_init = 0.8 - 0.6 * math.exp(-0.3 * l)
        mod8 = mod_all[l]
        lw = _layer_weights(w_in[l], mla_q_norm_g[l], mla_kv_norm_g[l], mla_w_uq[l], mla_w_ukv[l])
        (ka, qta, vta, kb, qtb, vtb, kc, qtc, vtc, dqkv, dz, dg) = _in_call(
            xu, mod8, norm1_g[l][None, :], lw, tabs, nct)

        sink_b = jnp.broadcast_to(jnp.pad(swa_sinks[l] * LOG2E, (0, 8 - SWA_HEADS))[:, None], (8, TILE))
        ya = _swa_call(qta, ka, vta, sink_b)
        g_b = jnp.broadcast_to(diff_norm_g[l][:, None], (DIFF_V_DIM, TILE))
        yb = _diff_call(qtb, kb, vtb, diff_lq1[l][None, :], diff_lk1[l][None, :], diff_lq2[l][None, :],
                        diff_lk2[l][None, :], g_b, lam_init, nct)
        yc = _mla_call(qtc, kc, vtc, nct)

        conv_w8 = jnp.pad(gdn_conv_w[l], ((0, 8 - GDN_CONV), (0, 0)))
        alog128 = jnp.pad(jnp.concatenate([gdn_a_log_f[l], gdn_a_log_b[l]]), (0, 128 - 2 * GDN_HEADS))[None, :]
        dtb128 = jnp.pad(jnp.concatenate([gdn_dt_bias_f[l], gdn_dt_bias_b[l]]), (0, 128 - 2 * GDN_HEADS))[None, :]
        gq, gk, gv, gb = _gdn_prep_call(dqkv, dg, conv_w8, alog128, dtb128, nct)
        of, ob = _gdn_scan_call(gq, gk, gv, gb, nct)

        gd = jnp.tile(gdn_norm_g[l], GDN_HEADS)[None, :]
        x1, h2, wt = _out_call(ya, yb, yc, of, ob, dz, xu, mod8, gd, w_out[l].astype(BF16), norm2_g[l][None, :],
                               moe_w_router[l].T, moe_bias[l][:, None], nct)
        xu = _moe_call(x1, h2, wt, mod8, moe_w_gate[l].astype(BF16), moe_w_up[l].astype(BF16),
                       moe_w_down[l].astype(BF16), shared_w_gate[l].astype(BF16), shared_w_up[l].astype(BF16),
                       shared_w_down[l].astype(BF16), final_norm_g[None, :], last, C)
    return xu[:, C:, :]
```

```python
import functools
import math

import jax
import jax.numpy as jnp
from jax import lax
from jax.experimental import pallas as pl
from jax.experimental.pallas import tpu as pltpu

F32 = jnp.float32
BF16 = jnp.bfloat16
HIGHEST = lax.Precision.HIGHEST

GRID_W = 64
EPS = 1e-6
ROPE_BASE = 10000.0

HEAD_DIM = 64
SWA_HEADS = 4
SWA_KV_HEADS = 2
WINDOW = 128
DIFF_HEADS = 4
DIFF_QK_DIM = 32
DIFF_V_DIM = 64
MLA_HEADS = 4
MLA_Q_RANK = 192
MLA_KV_RANK = 128
MLA_NOPE = 64
MLA_ROPE = 32
MLA_V = 64
GDN_HEADS = 4
GDN_DK = 64
GDN_DV = 64
GDN_CONV = 5
GDN_CHUNK = 64
GDN_QKV = 2 * GDN_HEADS * GDN_DK + GDN_HEADS * GDN_DV
GDN_CHUNKS_PER_STEP = 4
N_EXPERTS = 64
N_GROUPS = 8
TOPK_GROUPS = 4
TOP_K = 6
EXPERT_FF = 256
ROUTED_SCALE = 2.5

IN_SPLITS = (
    SWA_HEADS * HEAD_DIM, SWA_KV_HEADS * HEAD_DIM, SWA_KV_HEADS * HEAD_DIM,
    DIFF_HEADS * 2 * DIFF_QK_DIM, DIFF_HEADS * 2 * DIFF_QK_DIM, DIFF_HEADS * DIFF_V_DIM,
    MLA_Q_RANK, MLA_KV_RANK, MLA_ROPE,
    GDN_QKV, GDN_HEADS * GDN_DV, GDN_HEADS, GDN_HEADS, GDN_HEADS, GDN_HEADS,
)

TILE = 256
LANES = 128
NEG = -1e30
LOG2E = math.log2(math.e)
VMEM_LIMIT = 56 * 1024 * 1024

_AK, _AKS, _BK, _BKS, _CQ, _CKV, _KR, _KRS, _DQKV, _DZ, _DG, _NTOK = (
    0, 128, 256, 512, 768, 1024, 1152, 1280, 1408, 2176, 2432, 2560)
_AQ, _AQS, _AV, _BQ, _BQS, _BV, _NTRN = 0, 256, 512, 640, 896, 1152, 1408


def _cparams(sem):
    return pltpu.CompilerParams(dimension_semantics=sem, vmem_limit_bytes=VMEM_LIMIT)


def _nt_dot(a, b, precision=None):
    return lax.dot_general(a, b, (((1,), (1,)), ((), ())), preferred_element_type=F32, precision=precision)


def _tn_dot(a, b):
    return lax.dot_general(a, b, (((0,), (0,)), ((), ())), preferred_element_type=F32)


def _dot(a, b, precision=None):
    return jnp.dot(a, b, preferred_element_type=F32, precision=precision)


def _split_dot(x, m):
    hi = x.astype(BF16)
    lo = (x - hi.astype(F32)).astype(BF16)
    return _dot(hi, m) + _dot(lo, m)


def _msplit_dot(m, x):
    x1 = x.astype(BF16)
    r1 = x - x1.astype(F32)
    x2 = r1.astype(BF16)
    x3 = (r1 - x2.astype(F32)).astype(BF16)
    return _dot(m, x1) + _dot(m, x2) + _dot(m, x3)


def _dot3(a, b):
    ah = a.astype(BF16)
    al = (a - ah.astype(F32)).astype(BF16)
    bh = b.astype(BF16)
    bl = (b - bh.astype(F32)).astype(BF16)
    return _dot(ah, bh) + _dot(ah, bl) + _dot(al, bh)


def _silu(x):
    return x * jax.nn.sigmoid(x)


def _group_ones(n, group):
    shift = group.bit_length() - 1
    r = lax.shift_right_logical(lax.broadcasted_iota(jnp.int32, (n, n), 0), shift)
    c = lax.shift_right_logical(lax.broadcasted_iota(jnp.int32, (n, n), 1), shift)
    return jnp.where(r == c, 1.0, 0.0).astype(BF16)


def _ada_kernel(s_ref, w_ref, b_ref, o_ref):
    s = _silu(s_ref[...])
    o_ref[0] = _dot(s, w_ref[0], precision=HIGHEST) + b_ref[0]


def _ada_call(s8, w_ada, b_ada):
    L, D, D6 = w_ada.shape
    nj = D6 // D
    return pl.pallas_call(
        _ada_kernel,
        out_shape=jax.ShapeDtypeStruct((L, 8, D6), F32),
        grid=(L, nj),
        in_specs=[
            pl.BlockSpec((8, D), lambda l, j: (0, 0)),
            pl.BlockSpec((1, D, D), lambda l, j: (l, 0, j)),
            pl.BlockSpec((1, 1, D), lambda l, j: (l, 0, j)),
        ],
        out_specs=pl.BlockSpec((1, 8, D), lambda l, j: (l, 0, j)),
        compiler_params=_cparams(("arbitrary", "arbitrary")),
        name="ada_mod",
    )(s8, w_ada, b_ada.reshape(L, 1, D6))


def _in_kernel(scales, x_ref, mod_ref, g1_ref, wtok_ref, wtrn_ref, ctok_ref, stok_ref, ctrn_ref, strn_ref,
               gq_ref, gkv_ref, wuq_ref, wuqs_ref, wkn_ref, epl_ref, wvt_ref,
               ka_ref, qta_ref, vta_ref, kb_ref, qtb_ref, vtb_ref, kc_ref, qtc_ref, vtc_ref,
               dqkv_ref, dz_ref, dg_ref):
    sc_a, sc_b, sc_c = scales
    x = x_ref[0]
    mod = mod_ref[0]
    h = x * lax.rsqrt(jnp.mean(x * x, axis=-1, keepdims=True) + EPS) * g1_ref[...]
    h = h * (1.0 + mod[1:2]) + mod[0:1]
    hb = h.astype(BF16)

    def tok(a, b):
        return _dot(hb, wtok_ref[:, a:b])

    def trn(a, b):
        return _nt_dot(wtrn_ref[a:b, :], hb)

    ct = ctok_ref[...]
    st = stok_ref[...]
    ka_ref[0] = (tok(_AK, _AKS) * ct[:, 0:128] + tok(_AKS, _BK) * st[:, 0:128]).astype(BF16)
    qta_ref[0, 0] = ((trn(_AQ, _AQS) * ctrn_ref[0, 0:256] + trn(_AQS, _AV) * strn_ref[0, 0:256]) * sc_a).astype(BF16)
    vta_ref[0, 0] = trn(_AV, _BQ).astype(BF16)
    kb_ref[0] = (tok(_BK, _BKS) * ct[:, 128:384] + tok(_BKS, _CQ) * st[:, 128:384]).astype(BF16)
    qtb_ref[0, 0] = ((trn(_BQ, _BQS) * ctrn_ref[0, 256:512] + trn(_BQS, _BV) * strn_ref[0, 256:512]) * sc_b).astype(BF16)
    vtb_ref[0, 0] = trn(_BV, _NTRN).astype(BF16)
    cq = tok(_CQ, _CKV)
    cqn = (cq * lax.rsqrt(jnp.sum(cq * cq, axis=-1, keepdims=True) * (1.0 / MLA_Q_RANK) + EPS) * gq_ref[...]).astype(BF16)
    qtc = _nt_dot(wuq_ref[...], cqn) * ctrn_ref[0, 512:1024] + _nt_dot(wuqs_ref[...], cqn) * strn_ref[0, 512:1024]
    qtc_ref[0, 0] = (qtc * sc_c).astype(BF16)
    ckv = tok(_CKV, _KR)
    ckvn = (ckv * lax.rsqrt(jnp.mean(ckv * ckv, axis=-1, keepdims=True) + EPS) * gkv_ref[...]).astype(BF16)
    krr = (tok(_KR, _KRS) * ct[:, 384:512] + tok(_KRS, _DQKV) * st[:, 384:512]).astype(BF16)
    kc_ref[0] = (_dot(ckvn, wkn_ref[...]) + _dot(krr, epl_ref[...])).astype(BF16)
    vtc_ref[0, 0] = _nt_dot(wvt_ref[...], ckvn).astype(BF16)
    dqkv_ref[0] = tok(_DQKV, _DZ)
    dz_ref[0] = tok(_DZ, _DG)
    dg_ref[0] = tok(_DG, _NTOK)


def _in_call(xu, mod8, g1, lw, tabs, nct):
    B, T, D = xu.shape
    nt = T // TILE
    tm = TILE

    def full(a):
        nd = a.ndim
        return pl.BlockSpec(a.shape, lambda b, t: (0,) * nd)

    def tokspec(w):
        return pl.BlockSpec((1, tm, w), lambda b, t: (b, t, 0))

    def trnspec(r):
        return pl.BlockSpec((1, 1, r, tm), lambda b, t: (b, t, 0, 0))

    scales = (HEAD_DIM ** -0.5 * LOG2E, DIFF_QK_DIM ** -0.5 * LOG2E, (MLA_NOPE + MLA_ROPE) ** -0.5 * LOG2E)
    weights = (lw["wtok"], lw["wtrn"])
    tables = (tabs["ctok"], tabs["stok"], tabs["ctrn"], tabs["strn"])
    mla = (lw["gq"], lw["gkv"], lw["wuq"], lw["wuqs"], lw["wkn"], lw["epl"], lw["wvt"])
    in_specs = [
        tokspec(D),
        pl.BlockSpec((1, 6, D), lambda b, t: (jnp.where(t < nct, 0, b + 1), 0, 0)),
        full(g1), full(weights[0]), full(weights[1]),
        pl.BlockSpec((tm, 512), lambda b, t: (t, 0)),
        pl.BlockSpec((tm, 512), lambda b, t: (t, 0)),
        pl.BlockSpec((1, 1024, tm), lambda b, t: (t, 0, 0)),
        pl.BlockSpec((1, 1024, tm), lambda b, t: (t, 0, 0)),
    ] + [full(a) for a in mla]
    out_shape = (
        jax.ShapeDtypeStruct((B, T, 128), BF16), jax.ShapeDtypeStruct((B, nt, 256, tm), BF16),
        jax.ShapeDtypeStruct((B, nt, 128, tm), BF16),
        jax.ShapeDtypeStruct((B, T, 256), BF16), jax.ShapeDtypeStruct((B, nt, 256, tm), BF16),
        jax.ShapeDtypeStruct((B, nt, 256, tm), BF16),
        jax.ShapeDtypeStruct((B, T, 512), BF16), jax.ShapeDtypeStruct((B, nt, 512, tm), BF16),
        jax.ShapeDtypeStruct((B, nt, 256, tm), BF16),
        jax.ShapeDtypeStruct((B, T, GDN_QKV), F32), jax.ShapeDtypeStruct((B, T, 256), F32),
        jax.ShapeDtypeStruct((B, T, 128), F32),
    )
    out_specs = (
        tokspec(128), trnspec(256), trnspec(128),
        tokspec(256), trnspec(256), trnspec(256),
        tokspec(512), trnspec(512), trnspec(256),
        tokspec(GDN_QKV), tokspec(256), tokspec(128),
    )
    return pl.pallas_call(
        functools.partial(_in_kernel, scales),
        out_shape=out_shape, grid=(B, nt), in_specs=in_specs, out_specs=out_specs,
        compiler_params=_cparams(("arbitrary", "arbitrary")),
        name="in_proj",
    )(xu, mod8, g1, *weights, *tables, *mla)


def _swa_kernel(nt, q_ref, k0_ref, k1_ref, k2_ref, k3_ref, v0_ref, v1_ref, v2_ref, v3_ref, sink_ref,
                o_ref, yt_ref):
    t = pl.program_id(1)
    tq = TILE
    r = lax.broadcasted_iota(jnp.int32, (tq, tq), 0)
    q = lax.broadcasted_iota(jnp.int32, (tq, tq), 1)
    d = r - q
    zero = jnp.zeros((tq, tq), F32)
    neg = jnp.full((tq, tq), NEG, F32)

    def gate(cond):
        return jnp.where(cond, 0.0, NEG).astype(F32)

    bias = (
        None,
        jnp.where(d >= WINDOW, zero, neg) + gate(t >= 2),
        jnp.where(jnp.abs(d) <= WINDOW, zero, neg) + gate(t >= 1),
        jnp.where(d <= -WINDOW, zero, neg) + gate(jnp.logical_and(t >= 1, t <= nt - 2)),
    )
    krefs = (k0_ref, k1_ref, k2_ref, k3_ref)
    vrefs = (v0_ref, v1_ref, v2_ref, v3_ref)
    zq = jnp.zeros((HEAD_DIM, tq), BF16)
    group = SWA_HEADS // SWA_KV_HEADS
    for h in range(SWA_HEADS):
        g = h // group
        qh = q_ref[0, 0, HEAD_DIM * h:HEAD_DIM * (h + 1), :]
        wq = jnp.concatenate([qh, zq], axis=0) if g == 0 else jnp.concatenate([zq, qh], axis=0)
        sink = sink_ref[h:h + 1, :]
        s = []
        m = sink
        for j in range(4):
            sj = _dot(krefs[j][0], wq)
            if bias[j] is not None:
                sj = sj + bias[j]
            s.append(sj)
            m = jnp.maximum(m, jnp.max(sj, axis=0, keepdims=True))
        l = jnp.exp2(sink - m)
        acc = jnp.zeros((HEAD_DIM, tq), F32)
        for j in range(4):
            p = jnp.exp2(s[j] - m)
            l = l + jnp.sum(p, axis=0, keepdims=True)
            acc = acc + _dot(vrefs[j][0, 0, HEAD_DIM * g:HEAD_DIM * (g + 1), :], p.astype(BF16))
        yt_ref[HEAD_DIM * h:HEAD_DIM * (h + 1), :] = acc / l
    o_ref[0] = yt_ref[...].T.astype(BF16)


def _swa_call(qta, ka, vta, sink_b):
    B, nt, R, tm = qta.shape
    T = nt * tm

    def kspec(f):
        return pl.BlockSpec((1, tm, 128), lambda b, t: (b, f(t), 0))

    def vspec(f):
        return pl.BlockSpec((1, 1, 128, tm), lambda b, t: (b, f(t), 0, 0))

    fs = (lambda t: 0, lambda t: jnp.maximum(t - 1, 0), lambda t: t, lambda t: jnp.minimum(t + 1, nt - 1))
    return pl.pallas_call(
        functools.partial(_swa_kernel, nt),
        out_shape=jax.ShapeDtypeStruct((B, T, 256), BF16),
        grid=(B, nt),
        in_specs=[pl.BlockSpec((1, 1, R, tm), lambda b, t: (b, t, 0, 0))]
        + [kspec(f) for f in fs] + [vspec(f) for f in fs]
        + [pl.BlockSpec(sink_b.shape, lambda b, t: (0, 0))],
        out_specs=pl.BlockSpec((1, tm, 256), lambda b, t: (b, t, 0)),
        scratch_shapes=[pltpu.VMEM((256, tm), F32)],
        compiler_params=_cparams(("arbitrary", "arbitrary")),
        name="swa_attn",
    )(qta, ka, ka, ka, ka, vta, vta, vta, vta, sink_b)


def _flash_maps(n_maps, n_pairs, get_k, get_wq, get_v, s_ref, p_ref, a_ref, l_ref, acc_ref):
    tq = TILE
    n_last = 2 * n_pairs

    def stage_scores(c, slot):
        cc = jnp.minimum(c, n_last)
        for j in range(n_maps):
            s_ref[slot, j] = _dot(get_k(cc, j), get_wq(j))

    def stage_softmax(slot, ms, ls):
        new_m, new_l = [], []
        for j in range(n_maps):
            s = s_ref[slot, j]
            m_new = jnp.maximum(ms[j], jnp.max(s, axis=0, keepdims=True))
            alpha = jnp.exp2(ms[j] - m_new)
            p = jnp.exp2(s - m_new)
            p_ref[slot, j] = p.astype(BF16)
            a_ref[slot, j, 0:1, :] = alpha
            new_l.append(alpha * ls[j] + jnp.sum(p, axis=0, keepdims=True))
            new_m.append(m_new)
        return tuple(new_m), tuple(new_l)

    def stage_values(c, slot):
        for j in range(n_maps):
            acc_ref[j] = a_ref[slot, j, 0:1, :] * acc_ref[j] + _dot(get_v(c, j), p_ref[slot, j])

    acc_ref[...] = jnp.zeros(acc_ref.shape, F32)
    init = (tuple(jnp.full((1, tq), NEG, F32) for _ in range(n_maps)),
            tuple(jnp.zeros((1, tq), F32) for _ in range(n_maps)))
    stage_scores(0, 0)
    stage_scores(1, 1)
    carry = stage_softmax(0, *init)

    def body(i, carry):
        c = 2 * i + 1
        stage_scores(c + 1, 0)
        carry = stage_softmax(1, *carry)
        stage_values(c - 1, 0)
        stage_scores(c + 2, 1)
        carry = stage_softmax(0, *carry)
        stage_values(c, 1)
        return carry

    _, ls = lax.fori_loop(0, n_pairs, body, carry)
    stage_values(n_last, 0)
    for j in range(n_maps):
        l_ref[j:j + 1, :] = ls[j]


def _diff_kernel(nct, nt, lam_init, q_ref, k_ref, v_ref, lq1_ref, lk1_ref, lq2_ref, lk2_ref, g_ref,
                 o_ref, qm_ref, s_ref, p_ref, a_ref, l_ref, acc_ref, yt_ref):
    t = pl.program_id(1)
    tq = TILE
    n_maps = 2 * DIFF_HEADS
    rows = lax.broadcasted_iota(jnp.int32, (LANES, tq), 0)
    per = LANES // DIFF_QK_DIM
    for j in range(n_maps):
        grp, sub = j // per, j % per
        qg = q_ref[0, 0, LANES * grp:LANES * (grp + 1), :]
        keep = jnp.logical_and(rows >= DIFF_QK_DIM * sub, rows < DIFF_QK_DIM * (sub + 1))
        qm_ref[j] = jnp.where(keep, qg, jnp.zeros_like(qg))

    def get_k(c, j):
        grp = j // per
        return k_ref[0, pl.ds(pl.multiple_of(c * tq, tq), tq), LANES * grp:LANES * (grp + 1)]

    def get_wq(j):
        return qm_ref[j]

    def get_v(c, j):
        h = j // 2
        return v_ref[0, c, DIFF_V_DIM * h:DIFF_V_DIM * (h + 1), :]

    n_pairs = jnp.where(t < nct, 0, (nt - nct) // 2)
    _flash_maps(n_maps, n_pairs, get_k, get_wq, get_v, s_ref, p_ref, a_ref, l_ref, acc_ref)

    lam =(jnp.exp(jnp.sum(lq1_ref[...] * lk1_ref[...], axis=-1, keepdims=True))
           - jnp.exp(jnp.sum(lq2_ref[...] * lk2_ref[...], axis=-1, keepdims=True)) + lam_init)
    for h in range(DIFF_HEADS):
        o1 = acc_ref[2 * h] / l_ref[2 * h:2 * h + 1, :]
        o2 = acc_ref[2 * h + 1] / l_ref[2 * h + 1:2 * h + 2, :]
        o = o1 - lam * o2
        on = o * lax.rsqrt(jnp.mean(o * o, axis=0, keepdims=True) + EPS) * g_ref[...]
        yt_ref[DIFF_V_DIM * h:DIFF_V_DIM * (h + 1), :] = on * (1.0 - lam_init)
    o_ref[0] = yt_ref[...].T.astype(BF16)


def _diff_call(qtb, kb, vtb, lq1, lk1, lq2, lk2, g_b, lam_init, nct):
    B, nt, R, tm = qtb.shape
    T = nt * tm
    n_maps = 2 * DIFF_HEADS
    small = [lq1, lk1, lq2, lk2, g_b]
    return pl.pallas_call(
        functools.partial(_diff_kernel, nct, nt, lam_init),
        out_shape=jax.ShapeDtypeStruct((B, T, 256), BF16),
        grid=(B, nt),
        in_specs=[
            pl.BlockSpec((1, 1, R, tm), lambda b, t: (b, t, 0, 0)),
            pl.BlockSpec((1, T, 256), lambda b, t: (b, 0, 0)),
            pl.BlockSpec((1, nt, 256, tm), lambda b, t: (b, 0, 0, 0)),
        ] + [pl.BlockSpec(a.shape, lambda b, t: (0, 0)) for a in small],
        out_specs=pl.BlockSpec((1, tm, 256), lambda b, t: (b, t, 0)),
        scratch_shapes=[
            pltpu.VMEM((n_maps, LANES, tm), BF16),
            pltpu.VMEM((2, n_maps, tm, tm), F32), pltpu.VMEM((2, n_maps, tm, tm), BF16),
            pltpu.VMEM((2, n_maps, 8, tm), F32), pltpu.VMEM((n_maps, tm), F32),
            pltpu.VMEM((n_maps, DIFF_V_DIM, tm), F32),
            pltpu.VMEM((256, tm), F32),
        ],
        compiler_params=_cparams(("arbitrary", "arbitrary")),
        name="diff_attn",
    )(qtb, kb, vtb, *small)


def _mla_kernel(nct, nt, q_ref, k_ref, v_ref, o_ref, s_ref, p_ref, a_ref, l_ref, acc_ref, yt_ref):
    t = pl.program_id(1)
    tq = TILE

    def get_k(c, j):
        return k_ref[0, pl.ds(pl.multiple_of(c * tq, tq), tq), LANES * j:LANES * (j + 1)]

    def get_wq(j):
        return q_ref[0, 0, LANES * j:LANES * (j + 1), :]

    def get_v(c, j):
        return v_ref[0, c, MLA_V * j:MLA_V * (j + 1), :]

    n_pairs = jnp.where(t < nct, 0, (nt - nct) // 2)
    _flash_maps(MLA_HEADS, n_pairs, get_k, get_wq, get_v, s_ref, p_ref, a_ref, l_ref, acc_ref)
    for h in range(MLA_HEADS):
        yt_ref[MLA_V * h:MLA_V * (h + 1), :] = acc_ref[h] / l_ref[h:h + 1, :]
    o_ref[0] = yt_ref[...].T.astype(BF16)


def _mla_call(qtc, kc, vtc, nct):
    B, nt, R, tm = qtc.shape
    T = nt * tm
    return pl.pallas_call(
        functools.partial(_mla_kernel, nct, nt),
        out_shape=jax.ShapeDtypeStruct((B, T, 256), BF16),
        grid=(B, nt),
        in_specs=[
            pl.BlockSpec((1, 1, R, tm), lambda b, t: (b, t, 0, 0)),
            pl.BlockSpec((1, T, 512), lambda b, t: (b, 0, 0)),
            pl.BlockSpec((1, nt, 256, tm), lambda b, t: (b, 0, 0, 0)),
        ],
        out_specs=pl.BlockSpec((1, tm, 256), lambda b, t: (b, t, 0)),
        scratch_shapes=[
            pltpu.VMEM((2, MLA_HEADS, tm, tm), F32), pltpu.VMEM((2, MLA_HEADS, tm, tm), BF16),
            pltpu.VMEM((2, MLA_HEADS, 8, tm), F32), pltpu.VMEM((8, tm), F32),
            pltpu.VMEM((MLA_HEADS, MLA_V, tm), F32),
            pltpu.VMEM((256, tm), F32),
        ],
        compiler_params=_cparams(("arbitrary", "arbitrary")),
        name="mla_attn",
    )(qtc, kc, vtc)


def _gdn_prep_kernel(nct, nt, x_ref, xp_ref, xn_ref, w_ref, alog_ref, dtb_ref, dg_ref,
                     q_ref, k_ref, v_ref, gb_ref, xe_ref):
    t = pl.program_id(1)
    tm = TILE
    pad = GDN_CONV // 2
    keep_prev = jnp.where(jnp.logical_or(t == 0, t == nct), 0.0, 1.0).astype(F32)
    keep_next = jnp.where(jnp.logical_or(t == nct - 1, t == nt - 1), 0.0, 1.0).astype(F32)
    xe_ref[0:8, :] = xp_ref[0] * keep_prev
    xe_ref[8:8 + tm, :] = x_ref[0]
    xe_ref[8 + tm:16 + tm, :] = xn_ref[0] * keep_next
    y = jnp.zeros((tm, GDN_QKV), F32)
    for j in range(GDN_CONV):
        y = y + w_ref[j:j + 1, :] * xe_ref[8 - pad + j:8 - pad + j + tm, :]
    y = _silu(y)
    hq = GDN_HEADS * GDN_DK
    ones = _group_ones(hq, GDN_DK)
    q = y[:, 0:hq]
    k = y[:, hq:2 * hq]
    q_ref[0] = q * lax.rsqrt(_split_dot(q * q, ones) + EPS) * (GDN_DK ** -0.5)
    k_ref[0] = k * lax.rsqrt(_split_dot(k * k, ones) + EPS)
    v_ref[0] = y[:, 2 * hq:]
    dg = dg_ref[0]
    lane = lax.broadcasted_iota(jnp.int32, dg.shape, 1)
    xs = dg + dtb_ref[...]
    softplus = jnp.maximum(xs, 0.0) + jnp.log1p(jnp.exp(-jnp.abs(xs)))
    gate = jnp.where(lane < 2 * GDN_HEADS, -jnp.exp(alog_ref[...]) * softplus, 0.0)
    shift = GDN_CHUNK.bit_length() - 1
    rr = lax.broadcasted_iota(jnp.int32, (tm, tm), 0)
    cc = lax.broadcasted_iota(jnp.int32, (tm, tm), 1)
    same = lax.shift_right_logical(rr, shift) == lax.shift_right_logical(cc, shift)
    lower = jnp.where(jnp.logical_and(same, cc <= rr), 1.0, 0.0).astype(BF16)
    upper = jnp.where(jnp.logical_and(same, cc >= rr), 1.0, 0.0).astype(BF16)
    gcum = jnp.where(lane < GDN_HEADS, _msplit_dot(lower, gate), _msplit_dot(upper, gate))
    gb_ref[0] = jnp.where(lane < 2 * GDN_HEADS, gcum, jax.nn.sigmoid(dg))


def _gdn_prep_call(dqkv, dg, conv_w8, alog128, dtb128, nct):
    B, T, W = dqkv.shape
    nt = T // TILE
    tm = TILE
    hb = tm // 8
    return pl.pallas_call(
        functools.partial(_gdn_prep_kernel, nct, nt),
        out_shape=(jax.ShapeDtypeStruct((B, T, 256), F32),) * 3 + (jax.ShapeDtypeStruct((B, T, 128), F32),),
        grid=(B, nt),
        in_specs=[
            pl.BlockSpec((1, tm, W), lambda b, t: (b, t, 0)),
            pl.BlockSpec((1, 8, W), lambda b, t: (b, jnp.maximum(t * hb - 1, 0), 0)),
            pl.BlockSpec((1, 8, W), lambda b, t: (b, jnp.minimum((t + 1) * hb, T // 8 - 1), 0)),
            pl.BlockSpec((8, W), lambda b, t: (0, 0)),
            pl.BlockSpec((1, 128), lambda b, t: (0, 0)),
            pl.BlockSpec((1, 128), lambda b, t: (0, 0)),
            pl.BlockSpec((1, tm, 128), lambda b, t: (b, t, 0)),
        ],
        out_specs=(pl.BlockSpec((1, tm, 256), lambda b, t: (b, t, 0)),) * 3
        + (pl.BlockSpec((1, tm, 128), lambda b, t: (b, t, 0)),),
        scratch_shapes=[pltpu.VMEM((tm + 16, W), F32)],
        compiler_params=_cparams(("arbitrary", "arbitrary")),
        name="gdn_prep",
    )(dqkv, dqkv, dqkv, conv_w8, alog128, dtb128, dg)


def _head_blocks(x, bd):
    return jnp.concatenate([x] * GDN_HEADS, axis=0) * bd


def _row_blocks(x):
    cs = GDN_CHUNK
    out = x[0:cs]
    for h in range(1, GDN_HEADS):
        out = out + x[cs * h:cs * (h + 1)]
    return out


def _gdn_chunk_kernel(q_ref, k_ref, v_ref, g_ref, e_ref, m_ref, *out_refs):
    cs = GDN_CHUNK
    w = GDN_HEADS * GDN_DK
    bdb = m_ref[0].astype(BF16)
    ones = jnp.ones((cs, cs), BF16)

    def cat_dot3(x, y):
        xh = x.astype(BF16)
        xl = (x - xh.astype(F32)).astype(BF16)
        yh = y.astype(BF16)
        yl = (y - yh.astype(F32)).astype(BF16)
        r = _dot(jnp.concatenate([xh, xl], axis=0), _head_blocks(yh, bdb))
        return r[0:cs] + r[cs:2 * cs] + _dot(xh, _head_blocks(yl, bdb))

    chains = []
    for c in range(GDN_CHUNKS_PER_STEP):
        rows = slice(cs * c, cs * (c + 1))
        q, k, v, gates = q_ref[0, rows, :], k_ref[0, rows, :], v_ref[0, rows, :], g_ref[0, rows, :]
        g1 = gates.astype(BF16)
        r1 = gates - g1.astype(F32)
        g2 = r1.astype(BF16)
        g3 = (r1 - g2.astype(F32)).astype(BF16)
        k_bd = _head_blocks(k.astype(BF16), bdb)
        for d in range(2):
            incl, strict = _row_blocks(m_ref[1 + 2 * d]), _row_blocks(m_ref[2 + 2 * d])
            eye = incl - strict
            e = e_ref[d]
            ge = _dot(g1, e) + _dot(g2, e) + _dot(g3, e)
            gcw = ge[:, 0:w]
            betaw = ge[:, w:2 * w]
            gcrow = _msplit_dot(ones, gcw * eye)
            decay = jnp.exp((gcw - gcrow) * incl) * incl
            last_row = cs - 1 if d == 0 else 0
            glw = gcw[last_row:last_row + 1, :]
            kb = k * betaw
            a = _nt_dot(kb.astype(BF16), k_bd) * decay * strict
            att = _nt_dot(q.astype(BF16), k_bd) * decay
            chains.append(dict(c=c, d=d, rows=rows, q=q, k=k, vb=v * betaw, kb=kb, gcw=gcw, glw=glw, att=att,
                               pw=a, tinv=eye - a))

    for _ in range(5):
        for ch in chains:
            ch["pw"] = cat_dot3(ch["pw"], ch["pw"])
        for ch in chains:
            ch["tinv"] = ch["tinv"] + cat_dot3(ch["tinv"], ch["pw"])

    for ch in chains:
        c, d, rows = ch["c"], ch["d"], ch["rows"]
        eg = jnp.exp(ch["gcw"])
        tcat = ch["tinv"].astype(BF16)
        u = _dot(tcat, _head_blocks(ch["vb"].astype(BF16), bdb))
        wmat = _dot(tcat, _head_blocks((ch["kb"] * eg).astype(BF16), bdb))
        wq_ref, u_ref, att_ref, kd_ref, eg_ref = out_refs[5 * d:5 * d + 5]
        wq_ref[0, c] = jnp.concatenate([wmat, ch["q"] * eg], axis=0).astype(BF16)
        u_ref[0, rows, :] = u
        att_ref[0, rows, :] = ch["att"].astype(BF16)
        kd_ref[0, rows, :] = (ch["k"] * jnp.exp(ch["glw"] - ch["gcw"])).astype(BF16)
        eg_ref[0, c] = jnp.broadcast_to(jnp.exp(ch["glw"]), (8, w))


def _gdn_consts():
    w = GDN_HEADS * GDN_DK
    cs = GDN_CHUNK
    assert GDN_DK == cs and GDN_DV == cs
    ch = jnp.arange(128)[:, None]
    col = jnp.arange(2 * w)[None, :]
    seg, hcol = col // w, (col % w) // GDN_DK
    es = []
    for goff in (0, GDN_HEADS):
        pick = jnp.where(seg == 0, goff + hcol, 2 * GDN_HEADS + goff + hcol)
        es.append((ch == pick).astype(BF16))
    r = jnp.arange(GDN_HEADS * cs)[:, None]
    c = jnp.arange(GDN_HEADS * cs)[None, :]
    same = (r // cs) == (c // cs)
    masks = [same, same & (r >= c), same & (r > c), same & (r <= c), same & (r < c)]
    return jnp.stack(es), jnp.stack([m.astype(F32) for m in masks])


def _gdn_chunk_call(q, k, v, gbc, e2, masks):
    B, T, W = q.shape
    cs = GDN_CHUNK
    nc = T // cs
    per = GDN_CHUNKS_PER_STEP
    assert nc % per == 0

    def tok(width):
        return pl.BlockSpec((1, per * cs, width), lambda b, s: (b, s, 0))

    def full(a):
        nd = a.ndim
        return pl.BlockSpec(a.shape, lambda b, s: (0,) * nd)

    one_dir_shapes = (
        jax.ShapeDtypeStruct((B, nc, 2 * cs, W), BF16), jax.ShapeDtypeStruct((B, T, W), F32),
        jax.ShapeDtypeStruct((B, T, W), BF16), jax.ShapeDtypeStruct((B, T, W), BF16),
        jax.ShapeDtypeStruct((B, nc, 8, W), F32))
    one_dir_specs = (
        pl.BlockSpec((1, per, 2 * cs, W), lambda b, s: (b, s, 0, 0)), tok(W), tok(W), tok(W),
        pl.BlockSpec((1, per, 8, W), lambda b, s: (b, s, 0, 0)))
    return pl.pallas_call(
        _gdn_chunk_kernel,
        out_shape=one_dir_shapes * 2,
        grid=(B, nc // per),
        in_specs=[tok(W), tok(W), tok(W), tok(128), full(e2), full(masks)],
        out_specs=one_dir_specs * 2,
        compiler_params=_cparams(("arbitrary", "arbitrary")),
        name="gdn_chunk",
    )(q, k, v, gbc, e2, masks)


def _gdn_scan_kernel(bd_ref, wqf_ref, uf_ref, atf_ref, kdf_ref, egf_ref, wqb_ref, ub_ref, atb_ref, kdb_ref, egb_ref,
                     of_ref, ob_ref, sf_ref, sb_ref):
    cs = GDN_CHUNK

    @pl.when(pl.program_id(1) == 0)
    def _():
        sf_ref[...] = jnp.zeros(sf_ref.shape, F32)
        sb_ref[...] = jnp.zeros(sb_ref.shape, F32)

    bd = bd_ref[0]
    dirs = ((wqf_ref, uf_ref, atf_ref, kdf_ref, egf_ref, of_ref, sf_ref),
            (wqb_ref, ub_ref, atb_ref, kdb_ref, egb_ref, ob_ref, sb_ref))
    states = [d[6][...] for d in dirs]
    ws = [_dot(d[0][0, 0], s.astype(BF16)) for d, s in zip(dirs, states)]
    v_new = [d[1][0] - x[0:cs] for d, x in zip(dirs, ws)]
    vnb = [x.astype(BF16) for x in v_new]
    intra = [_dot(d[2][0], _head_blocks(x, bd).astype(BF16)) for d, x in zip(dirs, v_new)]
    upd = [_tn_dot(d[3][0], x) for d, x in zip(dirs, vnb)]
    for d, s, x, y, z in zip(dirs, states, ws, intra, upd):
        d[5][0] = x[cs:2 * cs] + y
        d[6][...] = s * d[4][0, 0][0:1, :] + z * bd


def _gdn_scan_call(chunk_outs, masks, nct):
    wqf, uf = chunk_outs[0], chunk_outs[1]
    B, T, W = uf.shape
    cs = GDN_CHUNK
    nc = T // cs
    ncc = nct * (TILE // cs)

    def fwd(b, s):
        return s

    def bwd(b, s):
        return jnp.where(s < ncc, ncc - 1 - s, nc - 1 - (s - ncc))

    specs = [pl.BlockSpec((1,) + masks.shape[1:], lambda b, s: (0, 0, 0))]
    for f in (fwd, bwd):
        specs += [
            pl.BlockSpec((1, 1, 2 * cs, W), lambda b, s, f=f: (b, f(b, s), 0, 0)),
            pl.BlockSpec((1, cs, W), lambda b, s, f=f: (b, f(b, s), 0)),
            pl.BlockSpec((1, cs, W), lambda b, s, f=f: (b, f(b, s), 0)),
            pl.BlockSpec((1, cs, W), lambda b, s, f=f: (b, f(b, s), 0)),
            pl.BlockSpec((1, 1, 8, W), lambda b, s, f=f: (b, f(b, s), 0, 0)),
        ]
    return pl.pallas_call(
        _gdn_scan_kernel,
        out_shape=(jax.ShapeDtypeStruct((B, T, W), F32),) * 2,
        grid=(B, nc),
        in_specs=specs,
        out_specs=(pl.BlockSpec((1, cs, W), lambda b, s: (b, fwd(b, s), 0)),
                   pl.BlockSpec((1, cs, W), lambda b, s: (b, bwd(b, s), 0))),
        scratch_shapes=[pltpu.VMEM((W, W), F32)] * 2,
        compiler_params=_cparams(("arbitrary", "arbitrary")),
        name="gdn_scan",
    )(masks, *chunk_outs)


def _route(logits_t, bias_col):
    ne, tm = logits_t.shape
    per = ne // N_GROUPS
    neg_inf = -jnp.inf
    scores = jax.nn.sigmoid(logits_t)
    choice = scores + bias_col
    idx_m = lax.broadcasted_iota(jnp.int32, (per, tm), 0).astype(F32)
    groups, gscore = [], []
    for g in range(N_GROUPS):
        cg = choice[per * g:per * (g + 1), :]
        m1 = jnp.max(cg, axis=0, keepdims=True)
        i1 = jnp.min(jnp.where(cg == m1, idx_m, float(per)), axis=0, keepdims=True)
        m2 = jnp.max(jnp.where(idx_m == i1, neg_inf, cg), axis=0, keepdims=True)
        groups.append(cg)
        gscore.append(m1 + m2)
    gsel = [jnp.zeros((1, tm), F32) for _ in range(N_GROUPS)]
    for _ in range(TOPK_GROUPS):
        gm = functools.reduce(jnp.maximum, gscore)
        found = jnp.zeros((1, tm), F32)
        for g in range(N_GROUPS):
            hit = jnp.logical_and(gscore[g] == gm, found < 0.5)
            gsel[g] = jnp.where(hit, 1.0, gsel[g])
            gscore[g] = jnp.where(hit, neg_inf, gscore[g])
            found = jnp.where(hit, 1.0, found)
    mc = jnp.concatenate([jnp.where(gsel[g] > 0.5, groups[g], neg_inf) for g in range(N_GROUPS)], axis=0)
    idx_e = lax.broadcasted_iota(jnp.int32, (ne, tm), 0).astype(F32)
    esel = jnp.zeros((ne, tm), F32)
    for _ in range(TOP_K):
        em = jnp.max(mc, axis=0, keepdims=True)
        ei = jnp.min(jnp.where(mc == em, idx_e, float(ne)), axis=0, keepdims=True)
        hit = idx_e == ei
        esel = jnp.where(hit, 1.0, esel)
        mc = jnp.where(hit, neg_inf, mc)
    picked = scores * esel
    return picked / jnp.sum(picked, axis=0, keepdims=True) * ROUTED_SCALE


def _out_kernel(ya_ref, yb_ref, yc_ref, of_ref, ob_ref, z_ref, x_ref, mod_ref, gd_ref, wo_ref, g2_ref,
                wr_ref, rb_ref, x1_ref, h2_ref, wt_ref):
    o = of_ref[0] + ob_ref[0]
    ones = _group_ones(GDN_HEADS * GDN_DV, GDN_DV)
    ms = _split_dot(o * o, ones) * (1.0 / GDN_DV)
    yd = o * lax.rsqrt(ms + EPS) * gd_ref[...] * _silu(z_ref[0])
    y = (_dot(ya_ref[0], wo_ref[0:256, :]) + _dot(yb_ref[0], wo_ref[256:512, :])
         + _dot(yc_ref[0], wo_ref[512:768, :]) + _dot(yd.astype(BF16), wo_ref[768:1024, :]))
    mod = mod_ref[0]
    x1 = x_ref[0] + mod[2:3] * y
    x1_ref[0] = x1
    h2 = x1 * lax.rsqrt(jnp.mean(x1 * x1, axis=-1, keepdims=True) + EPS) * g2_ref[...]
    h2 = h2 * (1.0 + mod[4:5]) + mod[3:4]
    h2_ref[0] = h2.astype(BF16)
    logits_t = _nt_dot(wr_ref[...], h2, precision=HIGHEST)
    w_t = _route(logits_t, rb_ref[...])
    w_pad = jnp.concatenate([w_t, jnp.zeros((LANES - N_EXPERTS, w_t.shape[1]), F32)], axis=0)
    wt_ref[0] = w_pad.T


def _out_call(ya, yb, yc, of, ob, dz, xu, mod8, gd, wo, g2, wr_t, rbias, nct):
    B, T, D = xu.shape
    nt = T // TILE
    tm = TILE

    def tokspec(w):
        return pl.BlockSpec((1, tm, w), lambda b, t: (b, t, 0))

    def full(a):
        nd = a.ndim
        return pl.BlockSpec(a.shape, lambda b, t: (0,) * nd)

    return pl.pallas_call(
        _out_kernel,
        out_shape=(jax.ShapeDtypeStruct((B, T, D), F32), jax.ShapeDtypeStruct((B, T, D), BF16),
                   jax.ShapeDtypeStruct((B, T, 128), F32)),
        grid=(B, nt),
        in_specs=[tokspec(256)] * 6 + [
            tokspec(D),
            pl.BlockSpec((1, 6, D), lambda b, t: (jnp.where(t < nct, 0, b + 1), 0, 0)),
            full(gd), full(wo), full(g2), full(wr_t), full(rbias),
        ],
        out_specs=(tokspec(D), tokspec(D), tokspec(128)),
        compiler_params=_cparams(("arbitrary", "arbitrary")),
        name="out_proj_route",
    )(ya, yb, yc, of, ob, dz, xu, mod8, gd, wo, g2, wr_t, rbias)


def _moe_kernel(last, n_ctx_rows, x1_ref, h_ref, wt_ref, modc_ref, modl_ref, wg_ref, wu_ref, wd_ref,
                sg_ref, su_ref, sd_ref, gf_ref, o_ref, acc_ref):
    i = pl.program_id(1)
    e = pl.program_id(2)
    tm = h_ref.shape[1]

    @pl.when(e == 0)
    def _():
        acc_ref[...] = jnp.zeros(acc_ref.shape, F32)

    @pl.when(e < N_EXPERTS)
    def _():
        x = h_ref[0]
        wt = wt_ref[0]
        lane = lax.broadcasted_iota(jnp.int32, wt.shape, 1)
        col = jnp.sum(jnp.where(lane == e, wt, 0.0), axis=-1, keepdims=True)
        hh = _silu(_dot(x, wg_ref[0])) * _dot(x, wu_ref[0]) * col
        acc_ref[...] += _dot(hh.astype(BF16), wd_ref[0])

    @pl.when(e == N_EXPERTS)
    def _():
        x = h_ref[0]
        hh = _silu(_dot(x, sg_ref[...])) * _dot(x, su_ref[...])
        f = acc_ref[...] + _dot(hh.astype(BF16), sd_ref[...])
        row = lax.broadcasted_iota(jnp.int32, (tm, 1), 0) + i * tm
        gate = jnp.where(row < n_ctx_rows, modc_ref[0][5:6], modl_ref[0][5:6])
        x2 = x1_ref[0] + gate * f
        if last:
            x2 = x2 * lax.rsqrt(jnp.mean(x2 * x2, axis=-1, keepdims=True) + EPS) * gf_ref[...]
        o_ref[0] = x2


def _moe_tile(T):
    best = LANES
    for cand in range(LANES, 1408 + 1, LANES):
        if T % cand == 0:
            best = cand
    return best


def _moe_call(x1, h2, wt, mod8, wg, wu, wd, sg, su, sd, gfin, last, n_ctx_rows):
    B, T, D = x1.shape
    tm = _moe_tile(T)
    F = wg.shape[-1]

    def tokspec(w):
        return pl.BlockSpec((1, tm, w), lambda b, i, e: (b, i, 0))

    def full(a):
        nd = a.ndim
        return pl.BlockSpec(a.shape, lambda b, i, e: (0,) * nd)

    def espec(shape):
        return pl.BlockSpec((1,) + shape, lambda b, i, e: (jnp.minimum(e, N_EXPERTS - 1), 0, 0))

    return pl.pallas_call(
        functools.partial(_moe_kernel, last, n_ctx_rows),
        out_shape=jax.ShapeDtypeStruct((B, T, D), F32),
        grid=(B, T // tm, N_EXPERTS + 1),
        in_specs=[
            tokspec(D), tokspec(D), tokspec(128),
            pl.BlockSpec((1, 6, D), lambda b, i, e: (0, 0, 0)),
            pl.BlockSpec((1, 6, D), lambda b, i, e: (b + 1, 0, 0)),
            espec((D, F)), espec((D, F)), espec((F, D)),
            full(sg), full(su), full(sd), full(gfin),
        ],
        out_specs=tokspec(D),
        scratch_shapes=[pltpu.VMEM((tm, D), F32)],
        compiler_params=_cparams(("arbitrary", "arbitrary", "arbitrary")),
        name="moe_dense",
    )(x1, h2, wt, mod8, mod8, wg, wu, wd, sg, su, sd, gfin)


def _swap_cols(w, dim):
    d0, n = w.shape[0], w.shape[1] // dim
    w4 = w.reshape(d0, n, 2, dim // 2)
    return jnp.concatenate([-w4[:, :, 1:], w4[:, :, :1]], axis=2).reshape(d0, n * dim)


def _pad_cols(w, width):
    return jnp.pad(w, ((0, 0), (0, width - w.shape[1])))


def _layer_weights(w_in, gq, gkv, w_uq, w_ukv):
    cuts, acc = [], 0
    for s in IN_SPLITS[:-1]:
        acc += s
        cuts.append(acc)
    (aq, ak, av, bq, bk, bv, cq, ckv, kr, dqkv, dz, daf, dab, dbf, dbb) = jnp.split(w_in, cuts, axis=-1)
    wtok = jnp.concatenate([
        ak, _swap_cols(ak, HEAD_DIM), bk, _swap_cols(bk, DIFF_QK_DIM),
        _pad_cols(cq, 256), ckv, _pad_cols(kr, 128), _pad_cols(_swap_cols(kr, MLA_ROPE), 128),
        dqkv, dz, _pad_cols(jnp.concatenate([daf, dab, dbf, dbb], axis=-1), 128),
    ], axis=-1).astype(BF16)
    wtrn = jnp.concatenate([
        aq, _swap_cols(aq, HEAD_DIM), av, bq, _swap_cols(bq, DIFF_QK_DIM), bv,
    ], axis=-1).T.astype(BF16)
    qd = MLA_NOPE + MLA_ROPE
    uq = w_uq.reshape(MLA_Q_RANK, MLA_HEADS, qd)
    uq_n, uq_r = uq[:, :, :MLA_NOPE], uq[:, :, MLA_NOPE:]
    uq_rs = jnp.concatenate([-uq_r[:, :, MLA_ROPE // 2:], uq_r[:, :, :MLA_ROPE // 2]], axis=-1)
    zpad = jnp.zeros((MLA_Q_RANK, MLA_HEADS, LANES - qd), F32)
    wuq = jnp.concatenate([uq_n, uq_r, zpad], axis=-1).reshape(MLA_Q_RANK, MLA_HEADS * LANES)
    wuqs = jnp.concatenate([jnp.zeros_like(uq_n), uq_rs, zpad], axis=-1).reshape(MLA_Q_RANK, MLA_HEADS * LANES)
    wuq = jnp.pad(wuq.T, ((0, 0), (0, 256 - MLA_Q_RANK))).astype(BF16)
    wuqs = jnp.pad(wuqs.T, ((0, 0), (0, 256 - MLA_Q_RANK))).astype(BF16)
    ukv = w_ukv.reshape(MLA_KV_RANK, MLA_HEADS, MLA_NOPE + MLA_V)
    wkn = jnp.concatenate([ukv[:, :, :MLA_NOPE], jnp.zeros((MLA_KV_RANK, MLA_HEADS, LANES - MLA_NOPE), F32)],
                          axis=-1).reshape(MLA_KV_RANK, MLA_HEADS * LANES).astype(BF16)
    wvt = ukv[:, :, MLA_NOPE:].reshape(MLA_KV_RANK, MLA_HEADS * MLA_V).T.astype(BF16)
    rr = jnp.arange(LANES)[:, None]
    cc = jnp.arange(MLA_HEADS * LANES)[None, :]
    epl = jnp.where(jnp.logical_and(rr < MLA_ROPE, (cc % LANES) == rr + MLA_NOPE), 1.0, 0.0).astype(BF16)
    return dict(wtok=wtok, wtrn=wtrn, gq=_pad_cols(gq[None, :], 256), gkv=gkv[None, :],
                wuq=wuq, wuqs=wuqs, wkn=wkn, epl=epl, wvt=wvt)


def _rope_full(S, C, dim):
    quarter = dim // 4
    inv = ROPE_BASE ** (-jnp.arange(quarter, dtype=F32) / quarter)
    rows = S // GRID_W
    r = jnp.repeat(jnp.arange(rows, dtype=F32), GRID_W)
    col = jnp.tile(jnp.arange(GRID_W, dtype=F32), rows)
    ang = jnp.concatenate([r[:, None] * inv, col[:, None] * inv], axis=-1)
    cos = jnp.concatenate([jnp.ones((C, dim // 2), F32), jnp.cos(ang)], axis=0)
    sin = jnp.concatenate([jnp.zeros((C, dim // 2), F32), jnp.sin(ang)], axis=0)
    return jnp.concatenate([cos, cos], axis=-1), jnp.concatenate([sin, sin], axis=-1)


def _rope_tables(S, C):
    T = S + C
    nt = T // TILE
    ca, sa = _rope_full(S, C, HEAD_DIM)
    cb, sb = _rope_full(S, C, DIFF_QK_DIM)
    cr, sr = _rope_full(S, C, MLA_ROPE)
    ctok = jnp.concatenate([jnp.tile(ca, (1, 2)), jnp.tile(cb, (1, 8)), _pad_cols(cr, 128)], axis=-1)
    stok = jnp.concatenate([jnp.tile(sa, (1, 2)), jnp.tile(sb, (1, 8)), _pad_cols(sr, 128)], axis=-1)
    one = jnp.ones((T, MLA_NOPE), F32)
    zero = jnp.zeros((T, MLA_NOPE), F32)
    padz = jnp.zeros((T, LANES - MLA_NOPE - MLA_ROPE), F32)
    cmla = jnp.tile(jnp.concatenate([one, cr, padz], axis=-1), (1, MLA_HEADS))
    smla = jnp.tile(jnp.concatenate([zero, sr, padz], axis=-1), (1, MLA_HEADS))
    ctrn = jnp.concatenate([jnp.tile(ca, (1, 4)), jnp.tile(cb, (1, 8)), cmla], axis=-1)
    strn = jnp.concatenate([jnp.tile(sa, (1, 4)), jnp.tile(sb, (1, 8)), smla], axis=-1)

    def blocked_t(a):
        return a.reshape(nt, TILE, a.shape[1]).transpose(0, 2, 1)

    return dict(ctok=ctok, stok=stok, ctrn=blocked_t(ctrn), strn=blocked_t(strn))


def kernel(x, c, ctx, c_ctx, w_ada, b_ada, norm1_g, norm2_g, w_in, w_out, swa_sinks, diff_lq1, diff_lk1, diff_lq2, diff_lk2, diff_norm_g, mla_q_norm_g, mla_kv_norm_g, mla_w_uq, mla_w_ukv, gdn_conv_w, gdn_a_log_f, gdn_a_log_b, gdn_dt_bias_f, gdn_dt_bias_b, gdn_norm_g, moe_w_router, moe_bias, moe_w_gate, moe_w_up, moe_w_down, shared_w_gate, shared_w_up, shared_w_down, final_norm_g):
    B, S, D = x.shape
    C = ctx.shape[1]
    L = w_ada.shape[0]
    assert S % (2 * TILE) == 0 and C == TILE and B + 1 <= 8 and S % GRID_W == 0
    T = S + C
    nct = C // TILE

    xu = jnp.concatenate([ctx, x], axis=1)
    s8 = jnp.zeros((8, D), F32).at[0].set(c_ctx).at[1:B + 1].set(c)
    mod_all = _ada_call(s8, w_ada, b_ada).reshape(L, 8, 6, D)
    tabs = _rope_tables(S, C)
    gdn_e, gdn_masks = _gdn_consts()

    for l in range(L):
        last = l == L - 1
        lam_init = 0.8 - 0.6 * math.exp(-0.3 * l)
        mod8 = mod_all[l]
        lw = _layer_weights(w_in[l], mla_q_norm_g[l], mla_kv_norm_g[l], mla_w_uq[l], mla_w_ukv[l])
        (ka, qta, vta, kb, qtb, vtb, kc, qtc, vtc, dqkv, dz, dg) = _in_call(
            xu, mod8, norm1_g[l][None, :], lw, tabs, nct)

        sink_b = jnp.broadcast_to(jnp.pad(swa_sinks[l] * LOG2E, (0, 8 - SWA_HEADS))[:, None], (8, TILE))
        ya = _swa_call(qta, ka, vta, sink_b)
        g_b = jnp.broadcast_to(diff_norm_g[l][:, None], (DIFF_V_DIM, TILE))
        yb = _diff_call(qtb, kb, vtb, diff_lq1[l][None, :], diff_lk1[l][None, :], diff_lq2[l][None, :],
                        diff_lk2[l][None, :], g_b, lam_init, nct)
        yc = _mla_call(qtc, kc, vtc, nct)

        conv_w8 = jnp.pad(gdn_conv_w[l], ((0, 8 - GDN_CONV), (0, 0)))
        alog128 = jnp.pad(jnp.concatenate([gdn_a_log_f[l], gdn_a_log_b[l]]), (0, 128 - 2 * GDN_HEADS))[None, :]
        dtb128 = jnp.pad(jnp.concatenate([gdn_dt_bias_f[l], gdn_dt_bias_b[l]]), (0, 128 - 2 * GDN_HEADS))[None, :]
        gq, gk, gv, gb = _gdn_prep_call(dqkv, dg, conv_w8, alog128, dtb128, nct)
        of, ob = _gdn_scan_call(_gdn_chunk_call(gq, gk, gv, gb, gdn_e, gdn_masks), gdn_masks, nct)

        gd = jnp.tile(gdn_norm_g[l], GDN_HEADS)[None, :]
        x1, h2, wt = _out_call(ya, yb, yc, of, ob, dz, xu, mod8, gd, w_out[l].astype(BF16), norm2_g[l][None, :],
                               moe_w_router[l].T, moe_bias[l][:, None], nct)
        xu = _moe_call(x1, h2, wt, mod8, moe_w_gate[l].astype(BF16), moe_w_up[l].astype(BF16),
                       moe_w_down[l].astype(BF16), shared_w_gate[l].astype(BF16), shared_w_up[l].astype(BF16),
                       shared_w_down[l].astype(BF16), final_norm_g[None, :], last, C)
    return xu[:, C:, :]
```

```python
import functools
import math

import jax
import jax.numpy as jnp
from jax import lax
from jax.experimental import pallas as pl
from jax.experimental.pallas import tpu as pltpu

F32 = jnp.float32
BF16 = jnp.bfloat16
HIGHEST = lax.Precision.HIGHEST

GRID_W = 64
EPS = 1e-6
ROPE_BASE = 10000.0

HEAD_DIM = 64
SWA_HEADS = 4
SWA_KV_HEADS = 2
WINDOW = 128
DIFF_HEADS = 4
DIFF_QK_DIM = 32
DIFF_V_DIM = 64
MLA_HEADS = 4
MLA_Q_RANK = 192
MLA_KV_RANK = 128
MLA_NOPE = 64
MLA_ROPE = 32
MLA_V = 64
GDN_HEADS = 4
GDN_DK = 64
GDN_DV = 64
GDN_CONV = 5
GDN_CHUNK = 64
GDN_QKV = 2 * GDN_HEADS * GDN_DK + GDN_HEADS * GDN_DV
GDN_CHUNKS_PER_STEP = 4
N_EXPERTS = 64
N_GROUPS = 8
TOPK_GROUPS = 4
TOP_K = 6
EXPERT_FF = 256
ROUTED_SCALE = 2.5

IN_SPLITS = (
    SWA_HEADS * HEAD_DIM, SWA_KV_HEADS * HEAD_DIM, SWA_KV_HEADS * HEAD_DIM,
    DIFF_HEADS * 2 * DIFF_QK_DIM, DIFF_HEADS * 2 * DIFF_QK_DIM, DIFF_HEADS * DIFF_V_DIM,
    MLA_Q_RANK, MLA_KV_RANK, MLA_ROPE,
    GDN_QKV, GDN_HEADS * GDN_DV, GDN_HEADS, GDN_HEADS, GDN_HEADS, GDN_HEADS,
)

TILE = 256
LANES = 128
NEG = -1e30
SUM_ROWS = 16
LOG2E = math.log2(math.e)
VMEM_LIMIT = 56 * 1024 * 1024
MOE_TILE_LIMIT = 1408

_AK, _AKS, _BK, _BKS, _CQ, _CKV, _KR, _KRS, _DQKV, _DZ, _DG, _NTOK = (
    0, 128, 256, 512, 768, 1024, 1152, 1280, 1408, 2176, 2432, 2560)
_AQ, _AQS, _AV, _BQ, _BQS, _BV, _NTRN = 0, 256, 512, 640, 896, 1152, 1408


def _cparams(sem):
    return pltpu.CompilerParams(dimension_semantics=sem, vmem_limit_bytes=VMEM_LIMIT)


def _nt_dot(a, b, precision=None):
    return lax.dot_general(a, b, (((1,), (1,)), ((), ())), preferred_element_type=F32, precision=precision)


def _tn_dot(a, b):
    return lax.dot_general(a, b, (((0,), (0,)), ((), ())), preferred_element_type=F32)


def _dot(a, b, precision=None):
    return jnp.dot(a, b, preferred_element_type=F32, precision=precision)


def _split_dot(x, m):
    hi = x.astype(BF16)
    lo = (x - hi.astype(F32)).astype(BF16)
    return _dot(hi, m) + _dot(lo, m)


def _msplit_dot(m, x):
    x1 = x.astype(BF16)
    r1 = x - x1.astype(F32)
    x2 = r1.astype(BF16)
    x3 = (r1 - x2.astype(F32)).astype(BF16)
    return _dot(m, x1) + _dot(m, x2) + _dot(m, x3)


def _dot3(a, b):
    ah = a.astype(BF16)
    al = (a - ah.astype(F32)).astype(BF16)
    bh = b.astype(BF16)
    bl = (b - bh.astype(F32)).astype(BF16)
    return _dot(ah, bh) + _dot(ah, bl) + _dot(al, bh)


def _silu(x):
    return x * jax.nn.sigmoid(x)


def _group_ones(n, group):
    shift = group.bit_length() - 1
    r = lax.shift_right_logical(lax.broadcasted_iota(jnp.int32, (n, n), 0), shift)
    c = lax.shift_right_logical(lax.broadcasted_iota(jnp.int32, (n, n), 1), shift)
    return jnp.where(r == c, 1.0, 0.0).astype(BF16)


def _ada_kernel(s_ref, w_ref, b_ref, o_ref):
    s = _silu(s_ref[...])
    o_ref[0] = _dot(s, w_ref[0], precision=HIGHEST) + b_ref[0]


def _ada_call(s8, w_ada, b_ada):
    L, D, D6 = w_ada.shape
    nj = D6 // D
    return pl.pallas_call(
        _ada_kernel,
        out_shape=jax.ShapeDtypeStruct((L, 8, D6), F32),
        grid=(L, nj),
        in_specs=[
            pl.BlockSpec((8, D), lambda l, j: (0, 0)),
            pl.BlockSpec((1, D, D), lambda l, j: (l, 0, j)),
            pl.BlockSpec((1, 1, D), lambda l, j: (l, 0, j)),
        ],
        out_specs=pl.BlockSpec((1, 8, D), lambda l, j: (l, 0, j)),
        compiler_params=_cparams(("arbitrary", "arbitrary")),
        name="ada_mod",
    )(s8, w_ada, b_ada.reshape(L, 1, D6))


def _in_kernel(scales, x_ref, mod_ref, g1_ref, wtok_ref, wtrn_ref, ctok_ref, stok_ref, ctrn_ref, strn_ref,
               gq_ref, gkv_ref, wuq_ref, wuqs_ref, wkn_ref, epl_ref, wvt_ref,
               ka_ref, qta_ref, vta_ref, kb_ref, qtb_ref, vtb_ref, kc_ref, qtc_ref, vtc_ref,
               dqkv_ref, dz_ref, dg_ref):
    sc_a, sc_b, sc_c = scales
    x = x_ref[0]
    mod = mod_ref[0]
    h = x * lax.rsqrt(jnp.mean(x * x, axis=-1, keepdims=True) + EPS) * g1_ref[...]
    h = h * (1.0 + mod[1:2]) + mod[0:1]
    hb = h.astype(BF16)

    def tok(a, b):
        return _dot(hb, wtok_ref[:, a:b])

    def trn(a, b):
        return _nt_dot(wtrn_ref[a:b, :], hb)

    ct = ctok_ref[...]
    st = stok_ref[...]
    ka_ref[0] = (tok(_AK, _AKS) * ct[:, 0:128] + tok(_AKS, _BK) * st[:, 0:128]).astype(BF16)
    qta_ref[0, 0] = ((trn(_AQ, _AQS) * ctrn_ref[0:256, :] + trn(_AQS, _AV) * strn_ref[0:256, :]) * sc_a).astype(BF16)
    vta_ref[0, 0] = trn(_AV, _BQ).astype(BF16)
    kb_ref[0] = (tok(_BK, _BKS) * ct[:, 128:384] + tok(_BKS, _CQ) * st[:, 128:384]).astype(BF16)
    qtb_ref[0, 0] = ((trn(_BQ, _BQS) * ctrn_ref[256:512, :] + trn(_BQS, _BV) * strn_ref[256:512, :]) * sc_b).astype(BF16)
    vtb_ref[0, 0] = trn(_BV, _NTRN).astype(BF16)
    cq = tok(_CQ, _CKV)
    cqn = (cq * lax.rsqrt(jnp.sum(cq * cq, axis=-1, keepdims=True) * (1.0 / MLA_Q_RANK) + EPS) * gq_ref[...]).astype(BF16)
    qtc = _nt_dot(wuq_ref[...], cqn) * ctrn_ref[512:1024, :] + _nt_dot(wuqs_ref[...], cqn) * strn_ref[512:1024, :]
    qtc_ref[0, 0] = (qtc * sc_c).astype(BF16)
    ckv = tok(_CKV, _KR)
    ckvn = (ckv * lax.rsqrt(jnp.mean(ckv * ckv, axis=-1, keepdims=True) + EPS) * gkv_ref[...]).astype(BF16)
    krr = (tok(_KR, _KRS) * ct[:, 384:512] + tok(_KRS, _DQKV) * st[:, 384:512]).astype(BF16)
    kc_ref[0] = (_dot(ckvn, wkn_ref[...]) + _dot(krr, epl_ref[...])).astype(BF16)
    vtc_ref[0, 0] = _nt_dot(wvt_ref[...], ckvn).astype(BF16)
    dqkv_ref[0] = tok(_DQKV, _DZ)
    dz_ref[0] = tok(_DZ, _DG)
    dg_ref[0] = tok(_DG, _NTOK)


def _in_call(xu, mod8, g1, lw, tabs, nct):
    B, T, D = xu.shape
    nt = T // TILE
    tm = TILE

    def full(a):
        nd = a.ndim
        return pl.BlockSpec(a.shape, lambda b, t: (0,) * nd)

    def tokspec(w):
        return pl.BlockSpec((1, tm, w), lambda b, t: (b, t, 0))

    def trnspec(r):
        return pl.BlockSpec((1, 1, r, tm), lambda b, t: (b, t, 0, 0))

    scales = (HEAD_DIM ** -0.5 * LOG2E, DIFF_QK_DIM ** -0.5 * LOG2E, (MLA_NOPE + MLA_ROPE) ** -0.5 * LOG2E)
    weights = (lw["wtok"], lw["wtrn"])
    tables = (tabs["ctok"], tabs["stok"], tabs["ctrn"], tabs["strn"])
    mla = (lw["gq"], lw["gkv"], lw["wuq"], lw["wuqs"], lw["wkn"], lw["epl"], lw["wvt"])
    in_specs = [
        tokspec(D),
        pl.BlockSpec((1, 6, D), lambda b, t: (jnp.where(t < nct, 0, b + 1), 0, 0)),
        full(g1), full(weights[0]), full(weights[1]),
        pl.BlockSpec((tm, 512), lambda b, t: (t, 0)),
        pl.BlockSpec((tm, 512), lambda b, t: (t, 0)),
        pl.BlockSpec((1024, tm), lambda b, t: (0, t)),
        pl.BlockSpec((1024, tm), lambda b, t: (0, t)),
    ] + [full(a) for a in mla]
    out_shape = (
        jax.ShapeDtypeStruct((B, T, 128), BF16), jax.ShapeDtypeStruct((B, nt, 256, tm), BF16),
        jax.ShapeDtypeStruct((B, nt, 128, tm), BF16),
        jax.ShapeDtypeStruct((B, T, 256), BF16), jax.ShapeDtypeStruct((B, nt, 256, tm), BF16),
        jax.ShapeDtypeStruct((B, nt, 256, tm), BF16),
        jax.ShapeDtypeStruct((B, T, 512), BF16), jax.ShapeDtypeStruct((B, nt, 512, tm), BF16),
        jax.ShapeDtypeStruct((B, nt, 256, tm), BF16),
        jax.ShapeDtypeStruct((B, T, GDN_QKV), F32), jax.ShapeDtypeStruct((B, T, 256), F32),
        jax.ShapeDtypeStruct((B, T, 128), F32),
    )
    out_specs = (
        tokspec(128), trnspec(256), trnspec(128),
        tokspec(256), trnspec(256), trnspec(256),
        tokspec(512), trnspec(512), trnspec(256),
        tokspec(GDN_QKV), tokspec(256), tokspec(128),
    )
    return pl.pallas_call(
        functools.partial(_in_kernel, scales),
        out_shape=out_shape, grid=(B, nt), in_specs=in_specs, out_specs=out_specs,
        compiler_params=_cparams(("arbitrary", "arbitrary")),
        name="in_proj",
    )(xu, mod8, g1, *weights, *tables, *mla)


def _swa_kernel(nt, q_ref, k0_ref, k1_ref, k2_ref, k3_ref, v0_ref, v1_ref, v2_ref, v3_ref, sink_ref,
                o_ref, yt_ref):
    t = pl.program_id(1)
    tq = TILE
    r = lax.broadcasted_iota(jnp.int32, (tq, tq), 0)
    q = lax.broadcasted_iota(jnp.int32, (tq, tq), 1)
    d = r - q
    zero = jnp.zeros((tq, tq), F32)
    neg = jnp.full((tq, tq), NEG, F32)

    def gate(cond):
        return jnp.where(cond, 0.0, NEG).astype(F32)

    bias = (
        None,
        jnp.where(d >= WINDOW, zero, neg) + gate(t >= 2),
        jnp.where(jnp.abs(d) <= WINDOW, zero, neg) + gate(t >= 1),
        jnp.where(d <= -WINDOW, zero, neg) + gate(jnp.logical_and(t >= 1, t <= nt - 2)),
    )
    krefs = (k0_ref, k1_ref, k2_ref, k3_ref)
    vrefs = (v0_ref, v1_ref, v2_ref, v3_ref)
    zq = jnp.zeros((HEAD_DIM, tq), BF16)
    group = SWA_HEADS // SWA_KV_HEADS
    for h in range(SWA_HEADS):
        g = h // group
        qh = q_ref[0, 0, HEAD_DIM * h:HEAD_DIM * (h + 1), :]
        wq = jnp.concatenate([qh, zq], axis=0) if g == 0 else jnp.concatenate([zq, qh], axis=0)
        sink = sink_ref[h:h + 1, :]
        s = []
        m = sink
        for j in range(4):
            sj = _dot(krefs[j][0], wq)
            if bias[j] is not None:
                sj = sj + bias[j]
            s.append(sj)
            m = jnp.maximum(m, jnp.max(sj, axis=0, keepdims=True))
        l = jnp.exp2(sink - m)
        acc = jnp.zeros((HEAD_DIM, tq), F32)
        for j in range(4):
            p = jnp.exp2(s[j] - m)
            l = l + jnp.sum(p, axis=0, keepdims=True)
            acc = acc + _dot(vrefs[j][0, 0, HEAD_DIM * g:HEAD_DIM * (g + 1), :], p.astype(BF16))
        yt_ref[HEAD_DIM * h:HEAD_DIM * (h + 1), :] = acc / l
    o_ref[0] = yt_ref[...].T.astype(BF16)


def _swa_call(qta, ka, vta, sink_b):
    B, nt, R, tm = qta.shape
    T = nt * tm

    def kspec(f):
        return pl.BlockSpec((1, tm, 128), lambda b, t: (b, f(t), 0))

    def vspec(f):
        return pl.BlockSpec((1, 1, 128, tm), lambda b, t: (b, f(t), 0, 0))

    fs = (lambda t: 0, lambda t: jnp.maximum(t - 1, 0), lambda t: t, lambda t: jnp.minimum(t + 1, nt - 1))
    return pl.pallas_call(
        functools.partial(_swa_kernel, nt),
        out_shape=jax.ShapeDtypeStruct((B, T, 256), BF16),
        grid=(B, nt),
        in_specs=[pl.BlockSpec((1, 1, R, tm), lambda b, t: (b, t, 0, 0))]
        + [kspec(f) for f in fs] + [vspec(f) for f in fs]
        + [pl.BlockSpec(sink_b.shape, lambda b, t: (0, 0))],
        out_specs=pl.BlockSpec((1, tm, 256), lambda b, t: (b, t, 0)),
        scratch_shapes=[pltpu.VMEM((256, tm), F32)],
        compiler_params=_cparams(("arbitrary", "arbitrary")),
        name="swa_attn",
    )(qta, ka, ka, ka, ka, vta, vta, vta, vta, sink_b)


def _flash_maps(n_maps, n_pairs, get_k, get_wq, get_v, s_ref, p_ref, a_ref, acc_ref):
    tq = TILE
    n_last = 2 * n_pairs
    ones = jnp.ones((SUM_ROWS, tq), BF16)

    def stage_scores(c, slot):
        cc = jnp.minimum(c, n_last)
        for j in range(n_maps):
            s_ref[slot, j] = _dot(get_k(cc, j), get_wq(j))

    def stage_softmax(slot, ms):
        new_m = []
        for j in range(n_maps):
            s = s_ref[slot, j]
            m_new = jnp.maximum(ms[j], jnp.max(s, axis=0, keepdims=True))
            p_ref[slot, j] = jnp.exp2(s - m_new).astype(BF16)
            a_ref[slot, j, 0:1, :] = jnp.exp2(ms[j] - m_new)
            new_m.append(m_new)
        return tuple(new_m)

    def stage_values(c, slot):
        for j in range(n_maps):
            v1 = jnp.concatenate([get_v(c, j), ones], axis=0)
            acc_ref[j] = a_ref[slot, j, 0:1, :] * acc_ref[j] + _dot(v1, p_ref[slot, j])

    acc_ref[...] = jnp.zeros(acc_ref.shape, F32)
    stage_scores(0, 0)
    stage_scores(1, 1)
    ms = stage_softmax(0, tuple(jnp.full((1, tq), NEG, F32) for _ in range(n_maps)))

    def body(i, ms):
        c = 2 * i + 1
        stage_scores(c + 1, 0)
        ms = stage_softmax(1, ms)
        stage_values(c - 1, 0)
        stage_scores(c + 2, 1)
        ms = stage_softmax(0, ms)
        stage_values(c, 1)
        return ms

    lax.fori_loop(0, n_pairs, body, ms)
    stage_values(n_last, 0)


def _diff_kernel(nct, nt, lam_init, q_ref, k_ref, v_ref, lq1_ref, lk1_ref, lq2_ref, lk2_ref, g_ref,
                 o_ref, qm_ref, s_ref, p_ref, a_ref, acc_ref, yt_ref):
    t = pl.program_id(1)
    tq = TILE
    n_maps = 2 * DIFF_HEADS
    rows = lax.broadcasted_iota(jnp.int32, (LANES, tq), 0)
    per = LANES // DIFF_QK_DIM
    for j in range(n_maps):
        grp, sub = j // per, j % per
        qg = q_ref[0, 0, LANES * grp:LANES * (grp + 1), :]
        keep = jnp.logical_and(rows >= DIFF_QK_DIM * sub, rows < DIFF_QK_DIM * (sub + 1))
        qm_ref[j] = jnp.where(keep, qg, jnp.zeros_like(qg))

    def get_k(c, j):
        grp = j // per
        return k_ref[0, pl.ds(pl.multiple_of(c * tq, tq), tq), LANES * grp:LANES * (grp + 1)]

    def get_wq(j):
        return qm_ref[j]

    def get_v(c, j):
        h = j // 2
        return v_ref[0, c, DIFF_V_DIM * h:DIFF_V_DIM * (h + 1), :]

    n_pairs = jnp.where(t < nct, 0, (nt - nct) // 2)
    _flash_maps(n_maps, n_pairs, get_k, get_wq, get_v, s_ref, p_ref, a_ref, acc_ref)

    lam =(jnp.exp(jnp.sum(lq1_ref[...] * lk1_ref[...], axis=-1, keepdims=True))
           - jnp.exp(jnp.sum(lq2_ref[...] * lk2_ref[...], axis=-1, keepdims=True)) + lam_init)
    for h in range(DIFF_HEADS):
        dv = DIFF_V_DIM
        o1 = acc_ref[2 * h, 0:dv, :] / acc_ref[2 * h, dv:dv + 1, :]
        o2 = acc_ref[2 * h + 1, 0:dv, :] / acc_ref[2 * h + 1, dv:dv + 1, :]
        o = o1 - lam * o2
        on = o * lax.rsqrt(jnp.mean(o * o, axis=0, keepdims=True) + EPS) * g_ref[...]
        yt_ref[DIFF_V_DIM * h:DIFF_V_DIM * (h + 1), :] = on * (1.0 - lam_init)
    o_ref[0] = yt_ref[...].T.astype(BF16)


def _diff_call(qtb, kb, vtb, lq1, lk1, lq2, lk2, g_b, lam_init, nct):
    B, nt, R, tm = qtb.shape
    T = nt * tm
    n_maps = 2 * DIFF_HEADS
    small = [lq1, lk1, lq2, lk2, g_b]
    return pl.pallas_call(
        functools.partial(_diff_kernel, nct, nt, lam_init),
        out_shape=jax.ShapeDtypeStruct((B, T, 256), BF16),
        grid=(B, nt),
        in_specs=[
            pl.BlockSpec((1, 1, R, tm), lambda b, t: (b, t, 0, 0)),
            pl.BlockSpec((1, T, 256), lambda b, t: (b, 0, 0)),
            pl.BlockSpec((1, nt, 256, tm), lambda b, t: (b, 0, 0, 0)),
        ] + [pl.BlockSpec(a.shape, lambda b, t: (0, 0)) for a in small],
        out_specs=pl.BlockSpec((1, tm, 256), lambda b, t: (b, t, 0)),
        scratch_shapes=[
            pltpu.VMEM((n_maps, LANES, tm), BF16),
            pltpu.VMEM((2, n_maps, tm, tm), F32), pltpu.VMEM((2, n_maps, tm, tm), BF16),
            pltpu.VMEM((2, n_maps, 8, tm), F32),
            pltpu.VMEM((n_maps, DIFF_V_DIM + SUM_ROWS, tm), F32),
            pltpu.VMEM((256, tm), F32),
        ],
        compiler_params=_cparams(("arbitrary", "arbitrary")),
        name="diff_attn",
    )(qtb, kb, vtb, *small)


def _mla_kernel(nct, nt, q_ref, k_ref, v_ref, o_ref, s_ref, p_ref, a_ref, acc_ref, yt_ref):
    t = pl.program_id(1)
    tq = TILE

    def get_k(c, j):
        return k_ref[0, pl.ds(pl.multiple_of(c * tq, tq), tq), LANES * j:LANES * (j + 1)]

    def get_wq(j):
        return q_ref[0, 0, LANES * j:LANES * (j + 1), :]

    def get_v(c, j):
        return v_ref[0, c, MLA_V * j:MLA_V * (j + 1), :]

    n_pairs = jnp.where(t < nct, 0, (nt - nct) // 2)
    _flash_maps(MLA_HEADS, n_pairs, get_k, get_wq, get_v, s_ref, p_ref, a_ref, acc_ref)
    for h in range(MLA_HEADS):
        yt_ref[MLA_V * h:MLA_V * (h + 1), :] = acc_ref[h, 0:MLA_V, :] / acc_ref[h, MLA_V:MLA_V + 1, :]
    o_ref[0] = yt_ref[...].T.astype(BF16)


def _mla_call(qtc, kc, vtc, nct):
    B, nt, R, tm = qtc.shape
    T = nt * tm
    return pl.pallas_call(
        functools.partial(_mla_kernel, nct, nt),
        out_shape=jax.ShapeDtypeStruct((B, T, 256), BF16),
        grid=(B, nt),
        in_specs=[
            pl.BlockSpec((1, 1, R, tm), lambda b, t: (b, t, 0, 0)),
            pl.BlockSpec((1, T, 512), lambda b, t: (b, 0, 0)),
            pl.BlockSpec((1, nt, 256, tm), lambda b, t: (b, 0, 0, 0)),
        ],
        out_specs=pl.BlockSpec((1, tm, 256), lambda b, t: (b, t, 0)),
        scratch_shapes=[
            pltpu.VMEM((2, MLA_HEADS, tm, tm), F32), pltpu.VMEM((2, MLA_HEADS, tm, tm), BF16),
            pltpu.VMEM((2, MLA_HEADS, 8, tm), F32),
            pltpu.VMEM((MLA_HEADS, MLA_V + SUM_ROWS, tm), F32),
            pltpu.VMEM((256, tm), F32),
        ],
        compiler_params=_cparams(("arbitrary", "arbitrary")),
        name="mla_attn",
    )(qtc, kc, vtc)


def _gdn_prep_kernel(nct, nt, x_ref, xp_ref, xn_ref, w_ref, alog_ref, dtb_ref, dg_ref,
                     q_ref, k_ref, v_ref, gb_ref, xe_ref):
    t = pl.program_id(1)
    tm = TILE
    pad = GDN_CONV // 2
    keep_prev = jnp.where(jnp.logical_or(t == 0, t == nct), 0.0, 1.0).astype(F32)
    keep_next = jnp.where(jnp.logical_or(t == nct - 1, t == nt - 1), 0.0, 1.0).astype(F32)
    xe_ref[0:8, :] = xp_ref[0] * keep_prev
    xe_ref[8:8 + tm, :] = x_ref[0]
    xe_ref[8 + tm:16 + tm, :] = xn_ref[0] * keep_next
    y = jnp.zeros((tm, GDN_QKV), F32)
    for j in range(GDN_CONV):
        y = y + w_ref[j:j + 1, :] * xe_ref[8 - pad + j:8 - pad + j + tm, :]
    y = _silu(y)
    hq = GDN_HEADS * GDN_DK
    ones = _group_ones(hq, GDN_DK)
    q = y[:, 0:hq]
    k = y[:, hq:2 * hq]
    q_ref[0] = q * lax.rsqrt(_split_dot(q * q, ones) + EPS) * (GDN_DK ** -0.5)
    k_ref[0] = k * lax.rsqrt(_split_dot(k * k, ones) + EPS)
    v_ref[0] = y[:, 2 * hq:]
    dg = dg_ref[0]
    lane = lax.broadcasted_iota(jnp.int32, dg.shape, 1)
    xs = dg + dtb_ref[...]
    softplus = jnp.maximum(xs, 0.0) + jnp.log1p(jnp.exp(-jnp.abs(xs)))
    gate = jnp.where(lane < 2 * GDN_HEADS, -jnp.exp(alog_ref[...]) * softplus, 0.0)
    shift = GDN_CHUNK.bit_length() - 1
    rr = lax.broadcasted_iota(jnp.int32, (tm, tm), 0)
    cc = lax.broadcasted_iota(jnp.int32, (tm, tm), 1)
    same = lax.shift_right_logical(rr, shift) == lax.shift_right_logical(cc, shift)
    lower = jnp.where(jnp.logical_and(same, cc <= rr), 1.0, 0.0).astype(BF16)
    upper = jnp.where(jnp.logical_and(same, cc >= rr), 1.0, 0.0).astype(BF16)
    gcum = jnp.where(lane < GDN_HEADS, _msplit_dot(lower, gate), _msplit_dot(upper, gate))
    gb_ref[0] = jnp.where(lane < 2 * GDN_HEADS, gcum, jax.nn.sigmoid(dg))


def _gdn_prep_call(dqkv, dg, conv_w8, alog128, dtb128, nct):
    B, T, W = dqkv.shape
    nt = T // TILE
    tm = TILE
    hb = tm // 8
    return pl.pallas_call(
        functools.partial(_gdn_prep_kernel, nct, nt),
        out_shape=(jax.ShapeDtypeStruct((B, T, 256), F32),) * 3 + (jax.ShapeDtypeStruct((B, T, 128), F32),),
        grid=(B, nt),
        in_specs=[
            pl.BlockSpec((1, tm, W), lambda b, t: (b, t, 0)),
            pl.BlockSpec((1, 8, W), lambda b, t: (b, jnp.maximum(t * hb - 1, 0), 0)),
            pl.BlockSpec((1, 8, W), lambda b, t: (b, jnp.minimum((t + 1) * hb, T // 8 - 1), 0)),
            pl.BlockSpec((8, W), lambda b, t: (0, 0)),
            pl.BlockSpec((1, 128), lambda b, t: (0, 0)),
            pl.BlockSpec((1, 128), lambda b, t: (0, 0)),
            pl.BlockSpec((1, tm, 128), lambda b, t: (b, t, 0)),
        ],
        out_specs=(pl.BlockSpec((1, tm, 256), lambda b, t: (b, t, 0)),) * 3
        + (pl.BlockSpec((1, tm, 128), lambda b, t: (b, t, 0)),),
        scratch_shapes=[pltpu.VMEM((tm + 16, W), F32)],
        compiler_params=_cparams(("arbitrary", "arbitrary")),
        name="gdn_prep",
    )(dqkv, dqkv, dqkv, conv_w8, alog128, dtb128, dg)


def _head_blocks(x, bd):
    return jnp.concatenate([x] * GDN_HEADS, axis=0) * bd


def _row_blocks(x):
    cs = GDN_CHUNK
    out = x[0:cs]
    for h in range(1, GDN_HEADS):
        out = out + x[cs * h:cs * (h + 1)]
    return out


def _gdn_chunk_kernel(q_ref, k_ref, v_ref, g_ref, e_ref, m_ref, *out_refs):
    cs = GDN_CHUNK
    w = GDN_HEADS * GDN_DK
    bdb = m_ref[0].astype(BF16)
    ones = jnp.ones((cs, cs), BF16)

    def cat_dot3(x, y):
        xh = x.astype(BF16)
        xl = (x - xh.astype(F32)).astype(BF16)
        yh = y.astype(BF16)
        yl = (y - yh.astype(F32)).astype(BF16)
        r = _dot(jnp.concatenate([xh, xl], axis=0), _head_blocks(yh, bdb))
        return r[0:cs] + r[cs:2 * cs] + _dot(xh, _head_blocks(yl, bdb))

    chains = []
    for c in range(GDN_CHUNKS_PER_STEP):
        rows = slice(cs * c, cs * (c + 1))
        q, k, v, gates = q_ref[0, rows, :], k_ref[0, rows, :], v_ref[0, rows, :], g_ref[0, rows, :]
        g1 = gates.astype(BF16)
        r1 = gates - g1.astype(F32)
        g2 = r1.astype(BF16)
        g3 = (r1 - g2.astype(F32)).astype(BF16)
        k_bd = _head_blocks(k.astype(BF16), bdb)
        for d in range(2):
            incl, strict = _row_blocks(m_ref[1 + 2 * d]), _row_blocks(m_ref[2 + 2 * d])
            eye = incl - strict
            e = e_ref[d]
            ge = _dot(g1, e) + _dot(g2, e) + _dot(g3, e)
            gcw = ge[:, 0:w]
            betaw = ge[:, w:2 * w]
            gcrow = _msplit_dot(ones, gcw * eye)
            decay = jnp.exp((gcw - gcrow) * incl) * incl
            last_row = cs - 1 if d == 0 else 0
            glw = gcw[last_row:last_row + 1, :]
            kb = k * betaw
            a = _nt_dot(kb.astype(BF16), k_bd) * decay * strict
            att = _nt_dot(q.astype(BF16), k_bd) * decay
            chains.append(dict(c=c, d=d, rows=rows, q=q, k=k, vb=v * betaw, kb=kb, gcw=gcw, glw=glw, att=att,
                               pw=a, tinv=eye - a))

    for _ in range(5):
        for ch in chains:
            ch["pw"] = cat_dot3(ch["pw"], ch["pw"])
        for ch in chains:
            ch["tinv"] = ch["tinv"] + cat_dot3(ch["tinv"], ch["pw"])

    for ch in chains:
        c, d, rows = ch["c"], ch["d"], ch["rows"]
        eg = jnp.exp(ch["gcw"])
        tcat = ch["tinv"].astype(BF16)
        u = _dot(tcat, _head_blocks(ch["vb"].astype(BF16), bdb))
        wmat = _dot(tcat, _head_blocks((ch["kb"] * eg).astype(BF16), bdb))
        wq_ref, u_ref, att_ref, kd_ref, eg_ref = out_refs[5 * d:5 * d + 5]
        wq_ref[0, c] = jnp.concatenate([wmat, ch["q"] * eg], axis=0).astype(BF16)
        u_ref[0, rows, :] = u
        att_ref[0, rows, :] = ch["att"].astype(BF16)
        kd_ref[0, rows, :] = (ch["k"] * jnp.exp(ch["glw"] - ch["gcw"])).astype(BF16)
        eg_ref[0, c] = jnp.broadcast_to(jnp.exp(ch["glw"]), (8, w))


def _gdn_consts():
    w = GDN_HEADS * GDN_DK
    cs = GDN_CHUNK
    assert GDN_DK == cs and GDN_DV == cs
    ch = jnp.arange(128)[:, None]
    col = jnp.arange(2 * w)[None, :]
    seg, hcol = col // w, (col % w) // GDN_DK
    es = []
    for goff in (0, GDN_HEADS):
        pick = jnp.where(seg == 0, goff + hcol, 2 * GDN_HEADS + goff + hcol)
        es.append((ch == pick).astype(BF16))
    r = jnp.arange(GDN_HEADS * cs)[:, None]
    c = jnp.arange(GDN_HEADS * cs)[None, :]
    same = (r // cs) == (c // cs)
    masks = [same, same & (r >= c), same & (r > c), same & (r <= c), same & (r < c)]
    return jnp.stack(es), jnp.stack([m.astype(F32) for m in masks])


def _gdn_chunk_call(q, k, v, gbc, e2, masks):
    B, T, W = q.shape
    cs = GDN_CHUNK
    nc = T // cs
    per = GDN_CHUNKS_PER_STEP
    assert nc % per == 0

    def tok(width):
        return pl.BlockSpec((1, per * cs, width), lambda b, s: (b, s, 0))

    def full(a):
        nd = a.ndim
        return pl.BlockSpec(a.shape, lambda b, s: (0,) * nd)

    one_dir_shapes = (
        jax.ShapeDtypeStruct((B, nc, 2 * cs, W), BF16), jax.ShapeDtypeStruct((B, T, W), F32),
        jax.ShapeDtypeStruct((B, T, W), BF16), jax.ShapeDtypeStruct((B, T, W), BF16),
        jax.ShapeDtypeStruct((B, nc, 8, W), F32))
    one_dir_specs = (
        pl.BlockSpec((1, per, 2 * cs, W), lambda b, s: (b, s, 0, 0)), tok(W), tok(W), tok(W),
        pl.BlockSpec((1, per, 8, W), lambda b, s: (b, s, 0, 0)))
    return pl.pallas_call(
        _gdn_chunk_kernel,
        out_shape=one_dir_shapes * 2,
        grid=(B, nc // per),
        in_specs=[tok(W), tok(W), tok(W), tok(128), full(e2), full(masks)],
        out_specs=one_dir_specs * 2,
        compiler_params=_cparams(("arbitrary", "arbitrary")),
        name="gdn_chunk",
    )(q, k, v, gbc, e2, masks)


def _gdn_scan_kernel(bd_ref, wqf_ref, uf_ref, atf_ref, kdf_ref, egf_ref, wqb_ref, ub_ref, atb_ref, kdb_ref, egb_ref,
                     of_ref, ob_ref, sf_ref, sb_ref):
    cs = GDN_CHUNK

    @pl.when(pl.program_id(1) == 0)
    def _():
        sf_ref[...] = jnp.zeros(sf_ref.shape, F32)
        sb_ref[...] = jnp.zeros(sb_ref.shape, F32)

    bd = bd_ref[0]
    dirs = ((wqf_ref, uf_ref, atf_ref, kdf_ref, egf_ref, of_ref, sf_ref),
            (wqb_ref, ub_ref, atb_ref, kdb_ref, egb_ref, ob_ref, sb_ref))
    states = [d[6][...] for d in dirs]
    ws = [_dot(d[0][0, 0], s.astype(BF16)) for d, s in zip(dirs, states)]
    v_new = [d[1][0] - x[0:cs] for d, x in zip(dirs, ws)]
    vnb = [x.astype(BF16) for x in v_new]
    intra = [_dot(d[2][0], _head_blocks(x, bd).astype(BF16)) for d, x in zip(dirs, v_new)]
    upd = [_tn_dot(d[3][0], x) for d, x in zip(dirs, vnb)]
    for d, s, x, y, z in zip(dirs, states, ws, intra, upd):
        d[5][0] = x[cs:2 * cs] + y
        d[6][...] = s * d[4][0, 0][0:1, :] + z * bd


def _gdn_scan_call(chunk_outs, masks, nct):
    wqf, uf = chunk_outs[0], chunk_outs[1]
    B, T, W = uf.shape
    cs = GDN_CHUNK
    nc = T // cs
    ncc = nct * (TILE // cs)

    def fwd(b, s):
        return s

    def bwd(b, s):
        return jnp.where(s < ncc, ncc - 1 - s, nc - 1 - (s - ncc))

    specs = [pl.BlockSpec((1,) + masks.shape[1:], lambda b, s: (0, 0, 0))]
    for f in (fwd, bwd):
        specs += [
            pl.BlockSpec((1, 1, 2 * cs, W), lambda b, s, f=f: (b, f(b, s), 0, 0)),
            pl.BlockSpec((1, cs, W), lambda b, s, f=f: (b, f(b, s), 0)),
            pl.BlockSpec((1, cs, W), lambda b, s, f=f: (b, f(b, s), 0)),
            pl.BlockSpec((1, cs, W), lambda b, s, f=f: (b, f(b, s), 0)),
            pl.BlockSpec((1, 1, 8, W), lambda b, s, f=f: (b, f(b, s), 0, 0)),
        ]
    return pl.pallas_call(
        _gdn_scan_kernel,
        out_shape=(jax.ShapeDtypeStruct((B, T, W), F32),) * 2,
        grid=(B, nc),
        in_specs=specs,
        out_specs=(pl.BlockSpec((1, cs, W), lambda b, s: (b, fwd(b, s), 0)),
                   pl.BlockSpec((1, cs, W), lambda b, s: (b, bwd(b, s), 0))),
        scratch_shapes=[pltpu.VMEM((W, W), F32)] * 2,
        compiler_params=_cparams(("arbitrary", "arbitrary")),
        name="gdn_scan",
    )(masks, *chunk_outs)


def _route(logits_t, bias_col):
    ne, tm = logits_t.shape
    per = ne // N_GROUPS
    neg_inf = -jnp.inf
    scores = jax.nn.sigmoid(logits_t)
    choice = scores + bias_col
    idx_m = lax.broadcasted_iota(jnp.int32, (per, tm), 0).astype(F32)
    groups, gscore = [], []
    for g in range(N_GROUPS):
        cg = choice[per * g:per * (g + 1), :]
        m1 = jnp.max(cg, axis=0, keepdims=True)
        i1 = jnp.min(jnp.where(cg == m1, idx_m, float(per)), axis=0, keepdims=True)
        m2 = jnp.max(jnp.where(idx_m == i1, neg_inf, cg), axis=0, keepdims=True)
        groups.append(cg)
        gscore.append(m1 + m2)
    gsel = [jnp.zeros((1, tm), F32) for _ in range(N_GROUPS)]
    for _ in range(TOPK_GROUPS):
        gm = functools.reduce(jnp.maximum, gscore)
        found = jnp.zeros((1, tm), F32)
        for g in range(N_GROUPS):
            hit = jnp.logical_and(gscore[g] == gm, found < 0.5)
            gsel[g] = jnp.where(hit, 1.0, gsel[g])
            gscore[g] = jnp.where(hit, neg_inf, gscore[g])
            found = jnp.where(hit, 1.0, found)
    mc = jnp.concatenate([jnp.where(gsel[g] > 0.5, groups[g], neg_inf) for g in range(N_GROUPS)], axis=0)
    idx_e = lax.broadcasted_iota(jnp.int32, (ne, tm), 0).astype(F32)
    esel = jnp.zeros((ne, tm), F32)
    for _ in range(TOP_K):
        em = jnp.max(mc, axis=0, keepdims=True)
        ei = jnp.min(jnp.where(mc == em, idx_e, float(ne)), axis=0, keepdims=True)
        hit = idx_e == ei
        esel = jnp.where(hit, 1.0, esel)
        mc = jnp.where(hit, neg_inf, mc)
    picked = scores * esel
    return picked / jnp.sum(picked, axis=0, keepdims=True) * ROUTED_SCALE


def _out_kernel(n_ctx_rows, ya_ref, yb_ref, yc_ref, of_ref, ob_ref, z_ref, x_ref, modc_ref, modl_ref, gd_ref, wo_ref,
                g2_ref, wr_ref, rb_ref, x1_ref, h2_ref, wt_ref):
    tm = x_ref.shape[1]
    row = lax.broadcasted_iota(jnp.int32, (tm, 1), 0) + pl.program_id(1) * tm
    is_ctx = row < n_ctx_rows
    modc, modl = modc_ref[0], modl_ref[0]

    def mod_row(i):
        return jnp.where(is_ctx, modc[i:i + 1], modl[i:i + 1])

    o = of_ref[0] + ob_ref[0]
    ones = _group_ones(GDN_HEADS * GDN_DV, GDN_DV)
    ms = _split_dot(o * o, ones) * (1.0 / GDN_DV)
    yd = o * lax.rsqrt(ms + EPS) * gd_ref[...] * _silu(z_ref[0])
    y = (_dot(ya_ref[0], wo_ref[0:256, :]) + _dot(yb_ref[0], wo_ref[256:512, :])
         + _dot(yc_ref[0], wo_ref[512:768, :]) + _dot(yd.astype(BF16), wo_ref[768:1024, :]))
    x1 = x_ref[0] + mod_row(2) * y
    x1_ref[0] = x1
    h2 = x1 * lax.rsqrt(jnp.mean(x1 * x1, axis=-1, keepdims=True) + EPS) * g2_ref[...]
    h2 = h2 * (1.0 + mod_row(4)) + mod_row(3)
    h2_ref[0] = h2.astype(BF16)
    logits_t = _nt_dot(wr_ref[...], h2, precision=HIGHEST)
    w_t = _route(logits_t, rb_ref[...])
    w_pad = jnp.concatenate([w_t, jnp.zeros((LANES - N_EXPERTS, w_t.shape[1]), F32)], axis=0)
    wt_ref[0] = w_pad.T


def _token_tile(T, step, limit):
    best = step
    for cand in range(step, limit + 1, step):
        if T % cand == 0:
            best = cand
    return best


def _out_call(ya, yb, yc, of, ob, dz, xu, mod8, gd, wo, g2, wr_t, rbias, n_ctx_rows):
    B, T, D = xu.shape
    tm = _token_tile(T, TILE, 3 * TILE)

    def tokspec(w):
        return pl.BlockSpec((1, tm, w), lambda b, t: (b, t, 0))

    def full(a):
        nd = a.ndim
        return pl.BlockSpec(a.shape, lambda b, t: (0,) * nd)

    return pl.pallas_call(
        functools.partial(_out_kernel, n_ctx_rows),
        out_shape=(jax.ShapeDtypeStruct((B, T, D), F32), jax.ShapeDtypeStruct((B, T, D), BF16),
                   jax.ShapeDtypeStruct((B, T, 128), F32)),
        grid=(B, T // tm),
        in_specs=[tokspec(256)] * 6 + [
            tokspec(D),
            pl.BlockSpec((1, 6, D), lambda b, t: (0, 0, 0)),
            pl.BlockSpec((1, 6, D), lambda b, t: (b + 1, 0, 0)),
            full(gd), full(wo), full(g2), full(wr_t), full(rbias),
        ],
        out_specs=(tokspec(D), tokspec(D), tokspec(128)),
        compiler_params=_cparams(("arbitrary", "arbitrary")),
        name="out_proj_route",
    )(ya, yb, yc, of, ob, dz, xu, mod8, mod8, gd, wo, g2, wr_t, rbias)


def _moe_kernel(last, n_ctx_rows, x1_ref, h_ref, wt_ref, modc_ref, modl_ref, wgu_ref, wd_ref,
                sgu_ref, sd_ref, gf_ref, o_ref, acc_ref):
    i = pl.program_id(1)
    e = pl.program_id(2)
    tm = h_ref.shape[1]
    ff = wd_ref.shape[1]

    def swiglu(x, w_gate_up):
        gu = _dot(x, w_gate_up)
        return _silu(gu[:, 0:ff]) * gu[:, ff:2 * ff]

    @pl.when(e == 0)
    def _():
        acc_ref[...] = jnp.zeros(acc_ref.shape, F32)

    @pl.when(e < N_EXPERTS)
    def _():
        x = h_ref[0]
        wt = wt_ref[0]
        lane = lax.broadcasted_iota(jnp.int32, wt.shape, 1)
        col = jnp.sum(jnp.where(lane == e, wt, 0.0), axis=-1, keepdims=True)
        hh = swiglu(x, wgu_ref[0]) * col
        acc_ref[...] += _dot(hh.astype(BF16), wd_ref[0])

    @pl.when(e == N_EXPERTS)
    def _():
        x = h_ref[0]
        f = acc_ref[...] + _dot(swiglu(x, sgu_ref[...]).astype(BF16), sd_ref[...])
        row = lax.broadcasted_iota(jnp.int32, (tm, 1), 0) + i * tm
        gate = jnp.where(row < n_ctx_rows, modc_ref[0][5:6], modl_ref[0][5:6])
        x2 = x1_ref[0] + gate * f
        if last:
            x2 = x2 * lax.rsqrt(jnp.mean(x2 * x2, axis=-1, keepdims=True) + EPS) * gf_ref[...]
        o_ref[0] = x2


def _moe_call(x1, h2, wt, mod8, wgu, wd, sgu, sd, gfin, last, n_ctx_rows):
    B, T, D = x1.shape
    tm = _token_tile(T, LANES, MOE_TILE_LIMIT)
    F = wd.shape[-2]

    def tokspec(w):
        return pl.BlockSpec((1, tm, w), lambda b, i, e: (b, i, 0))

    def full(a):
        nd = a.ndim
        return pl.BlockSpec(a.shape, lambda b, i, e: (0,) * nd)

    def espec(shape):
        return pl.BlockSpec((1,) + shape, lambda b, i, e: (jnp.minimum(e, N_EXPERTS - 1), 0, 0))

    return pl.pallas_call(
        functools.partial(_moe_kernel, last, n_ctx_rows),
        out_shape=jax.ShapeDtypeStruct((B, T, D), F32),
        grid=(B, T // tm, N_EXPERTS + 1),
        in_specs=[
            tokspec(D), tokspec(D), tokspec(128),
            pl.BlockSpec((1, 6, D), lambda b, i, e: (0, 0, 0)),
            pl.BlockSpec((1, 6, D), lambda b, i, e: (b + 1, 0, 0)),
            espec((D, 2 * F)), espec((F, D)),
            full(sgu), full(sd), full(gfin),
        ],
        out_specs=tokspec(D),
        scratch_shapes=[pltpu.VMEM((tm, D), F32)],
        compiler_params=_cparams(("arbitrary", "arbitrary", "arbitrary")),
        name="moe_dense",
    )(x1, h2, wt, mod8, mod8, wgu, wd, sgu, sd, gfin)


def _swap_cols(w, dim):
    d0, n = w.shape[0], w.shape[1] // dim
    w4 = w.reshape(d0, n, 2, dim // 2)
    return jnp.concatenate([-w4[:, :, 1:], w4[:, :, :1]], axis=2).reshape(d0, n * dim)


def _pad_cols(w, width):
    return jnp.pad(w, ((0, 0), (0, width - w.shape[1])))


def _layer_weights(w_in, gq, gkv, w_uq, w_ukv):
    cuts, acc = [], 0
    for s in IN_SPLITS[:-1]:
        acc += s
        cuts.append(acc)
    (aq, ak, av, bq, bk, bv, cq, ckv, kr, dqkv, dz, daf, dab, dbf, dbb) = jnp.split(w_in, cuts, axis=-1)
    wtok = jnp.concatenate([
        ak, _swap_cols(ak, HEAD_DIM), bk, _swap_cols(bk, DIFF_QK_DIM),
        _pad_cols(cq, 256), ckv, _pad_cols(kr, 128), _pad_cols(_swap_cols(kr, MLA_ROPE), 128),
        dqkv, dz, _pad_cols(jnp.concatenate([daf, dab, dbf, dbb], axis=-1), 128),
    ], axis=-1).astype(BF16)
    wtrn = jnp.concatenate([
        aq, _swap_cols(aq, HEAD_DIM), av, bq, _swap_cols(bq, DIFF_QK_DIM), bv,
    ], axis=-1).T.astype(BF16)
    qd = MLA_NOPE + MLA_ROPE
    uq = w_uq.reshape(MLA_Q_RANK, MLA_HEADS, qd)
    uq_n, uq_r = uq[:, :, :MLA_NOPE], uq[:, :, MLA_NOPE:]
    uq_rs = jnp.concatenate([-uq_r[:, :, MLA_ROPE // 2:], uq_r[:, :, :MLA_ROPE // 2]], axis=-1)
    zpad = jnp.zeros((MLA_Q_RANK, MLA_HEADS, LANES - qd), F32)
    wuq = jnp.concatenate([uq_n, uq_r, zpad], axis=-1).reshape(MLA_Q_RANK, MLA_HEADS * LANES)
    wuqs = jnp.concatenate([jnp.zeros_like(uq_n), uq_rs, zpad], axis=-1).reshape(MLA_Q_RANK, MLA_HEADS * LANES)
    wuq = jnp.pad(wuq.T, ((0, 0), (0, 256 - MLA_Q_RANK))).astype(BF16)
    wuqs = jnp.pad(wuqs.T, ((0, 0), (0, 256 - MLA_Q_RANK))).astype(BF16)
    ukv = w_ukv.reshape(MLA_KV_RANK, MLA_HEADS, MLA_NOPE + MLA_V)
    wkn = jnp.concatenate([ukv[:, :, :MLA_NOPE], jnp.zeros((MLA_KV_RANK, MLA_HEADS, LANES - MLA_NOPE), F32)],
                          axis=-1).reshape(MLA_KV_RANK, MLA_HEADS * LANES).astype(BF16)
    wvt = ukv[:, :, MLA_NOPE:].reshape(MLA_KV_RANK, MLA_HEADS * MLA_V).T.astype(BF16)
    rr = jnp.arange(LANES)[:, None]
    cc = jnp.arange(MLA_HEADS * LANES)[None, :]
    epl = jnp.where(jnp.logical_and(rr < MLA_ROPE, (cc % LANES) == rr + MLA_NOPE), 1.0, 0.0).astype(BF16)
    return dict(wtok=wtok, wtrn=wtrn, gq=_pad_cols(gq[None, :], 256), gkv=gkv[None, :],
                wuq=wuq, wuqs=wuqs, wkn=wkn, epl=epl, wvt=wvt)


def _rope_full(S, C, dim, feature_major):
    quarter = dim // 4
    inv = ROPE_BASE ** (-jnp.arange(quarter, dtype=F32) / quarter)
    rows = S // GRID_W
    r = jnp.repeat(jnp.arange(rows, dtype=F32), GRID_W)
    col = jnp.tile(jnp.arange(GRID_W, dtype=F32), rows)
    if feature_major:
        ang = jnp.concatenate([inv[:, None] * r[None, :], inv[:, None] * col[None, :]], axis=0)
        cos = jnp.concatenate([jnp.ones((dim // 2, C), F32), jnp.cos(ang)], axis=1)
        sin = jnp.concatenate([jnp.zeros((dim // 2, C), F32), jnp.sin(ang)], axis=1)
        return jnp.concatenate([cos, cos], axis=0), jnp.concatenate([sin, sin], axis=0)
    ang = jnp.concatenate([r[:, None] * inv, col[:, None] * inv], axis=-1)
    cos = jnp.concatenate([jnp.ones((C, dim // 2), F32), jnp.cos(ang)], axis=0)
    sin = jnp.concatenate([jnp.zeros((C, dim // 2), F32), jnp.sin(ang)], axis=0)
    return jnp.concatenate([cos, cos], axis=-1), jnp.concatenate([sin, sin], axis=-1)


def _rope_tables(S, C):
    T = S + C
    ca, sa = _rope_full(S, C, HEAD_DIM, False)
    cb, sb = _rope_full(S, C, DIFF_QK_DIM, False)
    cr, sr = _rope_full(S, C, MLA_ROPE, False)
    ctok = jnp.concatenate([jnp.tile(ca, (1, 2)), jnp.tile(cb, (1, 8)), _pad_cols(cr, 128)], axis=-1)
    stok = jnp.concatenate([jnp.tile(sa, (1, 2)), jnp.tile(sb, (1, 8)), _pad_cols(sr, 128)], axis=-1)
    cat, sat = _rope_full(S, C, HEAD_DIM, True)
    cbt, sbt = _rope_full(S, C, DIFF_QK_DIM, True)
    crt, srt = _rope_full(S, C, MLA_ROPE, True)
    one = jnp.ones((MLA_NOPE, T), F32)
    zero = jnp.zeros((MLA_NOPE, T), F32)
    padz = jnp.zeros((LANES - MLA_NOPE - MLA_ROPE, T), F32)
    cmla = jnp.tile(jnp.concatenate([one, crt, padz], axis=0), (MLA_HEADS, 1))
    smla = jnp.tile(jnp.concatenate([zero, srt, padz], axis=0), (MLA_HEADS, 1))
    ctrn = jnp.concatenate([jnp.tile(cat, (4, 1)), jnp.tile(cbt, (8, 1)), cmla], axis=0)
    strn = jnp.concatenate([jnp.tile(sat, (4, 1)), jnp.tile(sbt, (8, 1)), smla], axis=0)
    return dict(ctok=ctok, stok=stok, ctrn=ctrn, strn=strn)


def kernel(x, c, ctx, c_ctx, w_ada, b_ada, norm1_g, norm2_g, w_in, w_out, swa_sinks, diff_lq1, diff_lk1, diff_lq2, diff_lk2, diff_norm_g, mla_q_norm_g, mla_kv_norm_g, mla_w_uq, mla_w_ukv, gdn_conv_w, gdn_a_log_f, gdn_a_log_b, gdn_dt_bias_f, gdn_dt_bias_b, gdn_norm_g, moe_w_router, moe_bias, moe_w_gate, moe_w_up, moe_w_down, shared_w_gate, shared_w_up, shared_w_down, final_norm_g):
    B, S, D = x.shape
    C = ctx.shape[1]
    L = w_ada.shape[0]
    assert S % (2 * TILE) == 0 and C == TILE and B + 1 <= 8 and S % GRID_W == 0
    T = S + C
    nct = C // TILE

    xu = jnp.concatenate([ctx, x], axis=1)
    s8 = jnp.zeros((8, D), F32).at[0].set(c_ctx).at[1:B + 1].set(c)
    mod_all = _ada_call(s8, w_ada, b_ada).reshape(L, 8, 6, D)
    tabs = _rope_tables(S, C)
    gdn_e, gdn_masks = _gdn_consts()

    for l in range(L):
        last = l == L - 1
        lam_init = 0.8 - 0.6 * math.exp(-0.3 * l)
        mod8 = mod_all[l]
        lw = _layer_weights(w_in[l], mla_q_norm_g[l], mla_kv_norm_g[l], mla_w_uq[l], mla_w_ukv[l])
        (ka, qta, vta, kb, qtb, vtb, kc, qtc, vtc, dqkv, dz, dg) = _in_call(
            xu, mod8, norm1_g[l][None, :], lw, tabs, nct)

        sink_b = jnp.broadcast_to(jnp.pad(swa_sinks[l] * LOG2E, (0, 8 - SWA_HEADS))[:, None], (8, TILE))
        ya = _swa_call(qta, ka, vta, sink_b)
        g_b = jnp.broadcast_to(diff_norm_g[l][:, None], (DIFF_V_DIM, TILE))
        yb = _diff_call(qtb, kb, vtb, diff_lq1[l][None, :], diff_lk1[l][None, :], diff_lq2[l][None, :],
                        diff_lk2[l][None, :], g_b, lam_init, nct)
        yc = _mla_call(qtc, kc, vtc, nct)

        conv_w8 = jnp.pad(gdn_conv_w[l], ((0, 8 - GDN_CONV), (0, 0)))
        alog128 = jnp.pad(jnp.concatenate([gdn_a_log_f[l], gdn_a_log_b[l]]), (0, 128 - 2 * GDN_HEADS))[None, :]
        dtb128 = jnp.pad(jnp.concatenate([gdn_dt_bias_f[l], gdn_dt_bias_b[l]]), (0, 128 - 2 * GDN_HEADS))[None, :]
        gq, gk, gv, gb = _gdn_prep_call(dqkv, dg, conv_w8, alog128, dtb128, nct)
        of, ob = _gdn_scan_call(_gdn_chunk_call(gq, gk, gv, gb, gdn_e, gdn_masks), gdn_masks, nct)

        gd = jnp.tile(gdn_norm_g[l], GDN_HEADS)[None, :]
        x1, h2, wt = _out_call(ya, yb, yc, of, ob, dz, xu, mod8, gd, w_out[l].astype(BF16), norm2_g[l][None, :],
                               moe_w_router[l].T, moe_bias[l][:, None], C)
        wgu = jnp.concatenate([moe_w_gate[l], moe_w_up[l]], axis=-1).astype(BF16)
        sgu = jnp.concatenate([shared_w_gate[l], shared_w_up[l]], axis=-1).astype(BF16)
        xu = _moe_call(x1, h2, wt, mod8, wgu, moe_w_down[l].astype(BF16), sgu, shared_w_down[l].astype(BF16),
                       final_norm_g[None, :], last, C)
    return xu[:, C:, :]
```

```python
import functools
import math

import jax
import jax.numpy as jnp
from jax import lax
from jax.experimental import pallas as pl
from jax.experimental.pallas import tpu as pltpu

F32 = jnp.float32
BF16 = jnp.bfloat16
HIGHEST = lax.Precision.HIGHEST

GRID_W = 64
EPS = 1e-6
ROPE_BASE = 10000.0

HEAD_DIM = 64
SWA_HEADS = 4
SWA_KV_HEADS = 2
WINDOW = 128
DIFF_HEADS = 4
DIFF_QK_DIM = 32
DIFF_V_DIM = 64
MLA_HEADS = 4
MLA_Q_RANK = 192
MLA_KV_RANK = 128
MLA_NOPE = 64
MLA_ROPE = 32
MLA_V = 64
GDN_HEADS = 4
GDN_DK = 64
GDN_DV = 64
GDN_CONV = 5
GDN_CHUNK = 64
GDN_QKV = 2 * GDN_HEADS * GDN_DK + GDN_HEADS * GDN_DV
GDN_CHUNKS_PER_STEP = 4
N_EXPERTS = 64
N_GROUPS = 8
TOPK_GROUPS = 4
TOP_K = 6
EXPERT_FF = 256
ROUTED_SCALE = 2.5

IN_SPLITS = (
    SWA_HEADS * HEAD_DIM, SWA_KV_HEADS * HEAD_DIM, SWA_KV_HEADS * HEAD_DIM,
    DIFF_HEADS * 2 * DIFF_QK_DIM, DIFF_HEADS * 2 * DIFF_QK_DIM, DIFF_HEADS * DIFF_V_DIM,
    MLA_Q_RANK, MLA_KV_RANK, MLA_ROPE,
    GDN_QKV, GDN_HEADS * GDN_DV, GDN_HEADS, GDN_HEADS, GDN_HEADS, GDN_HEADS,
)

TILE = 256
LANES = 128
NEG = -1e30
SUM_ROWS = 16
LOG2E = math.log2(math.e)
VMEM_LIMIT = 56 * 1024 * 1024
MOE_TILE_LIMIT = 768
MOE_EXPERTS_PER_STEP = 2

_AK, _AKS, _BK, _BKS, _CQ, _CKV, _KR, _KRS, _DQKV, _DZ, _DG, _NTOK = (
    0, 128, 256, 512, 768, 1024, 1152, 1280, 1408, 2176, 2432, 2560)
_AQ, _AQS, _AV, _BQ, _BQS, _BV, _NTRN = 0, 256, 512, 640, 896, 1152, 1408


def _cparams(sem):
    return pltpu.CompilerParams(dimension_semantics=sem, vmem_limit_bytes=VMEM_LIMIT)


def _nt_dot(a, b, precision=None):
    return lax.dot_general(a, b, (((1,), (1,)), ((), ())), preferred_element_type=F32, precision=precision)


def _tn_dot(a, b):
    return lax.dot_general(a, b, (((0,), (0,)), ((), ())), preferred_element_type=F32)


def _dot(a, b, precision=None):
    return jnp.dot(a, b, preferred_element_type=F32, precision=precision)


def _split_dot(x, m):
    hi = x.astype(BF16)
    lo = (x - hi.astype(F32)).astype(BF16)
    return _dot(hi, m) + _dot(lo, m)


def _msplit_dot(m, x):
    x1 = x.astype(BF16)
    r1 = x - x1.astype(F32)
    x2 = r1.astype(BF16)
    x3 = (r1 - x2.astype(F32)).astype(BF16)
    return _dot(m, x1) + _dot(m, x2) + _dot(m, x3)


def _dot3(a, b):
    ah = a.astype(BF16)
    al = (a - ah.astype(F32)).astype(BF16)
    bh = b.astype(BF16)
    bl = (b - bh.astype(F32)).astype(BF16)
    return _dot(ah, bh) + _dot(ah, bl) + _dot(al, bh)


def _silu(x):
    return x * jax.nn.sigmoid(x)


def _group_ones(n, group):
    shift = group.bit_length() - 1
    r = lax.shift_right_logical(lax.broadcasted_iota(jnp.int32, (n, n), 0), shift)
    c = lax.shift_right_logical(lax.broadcasted_iota(jnp.int32, (n, n), 1), shift)
    return jnp.where(r == c, 1.0, 0.0).astype(BF16)


def _ada_kernel(s_ref, w_ref, b_ref, o_ref):
    s = _silu(s_ref[...])
    o_ref[0] = _dot(s, w_ref[0], precision=HIGHEST) + b_ref[0]


def _ada_call(s8, w_ada, b_ada):
    L, D, D6 = w_ada.shape
    nj = D6 // D
    return pl.pallas_call(
        _ada_kernel,
        out_shape=jax.ShapeDtypeStruct((L, 8, D6), F32),
        grid=(L, nj),
        in_specs=[
            pl.BlockSpec((8, D), lambda l, j: (0, 0)),
            pl.BlockSpec((1, D, D), lambda l, j: (l, 0, j)),
            pl.BlockSpec((1, 1, D), lambda l, j: (l, 0, j)),
        ],
        out_specs=pl.BlockSpec((1, 8, D), lambda l, j: (l, 0, j)),
        compiler_params=_cparams(("arbitrary", "arbitrary")),
        name="ada_mod",
    )(s8, w_ada, b_ada.reshape(L, 1, D6))


def _in_kernel(scales, x_ref, mod_ref, g1_ref, wtok_ref, wtrn_ref, ctok_ref, stok_ref, ctrn_ref, strn_ref,
               gq_ref, gkv_ref, wuq_ref, wuqs_ref, wkn_ref, epl_ref, wvt_ref,
               ka_ref, qta_ref, vta_ref, kb_ref, qtb_ref, vtb_ref, kc_ref, qtc_ref, vtc_ref,
               dqkv_ref, dz_ref, dg_ref):
    sc_a, sc_b, sc_c = scales
    x = x_ref[0]
    mod = mod_ref[0]
    h = x * lax.rsqrt(jnp.mean(x * x, axis=-1, keepdims=True) + EPS) * g1_ref[...]
    h = h * (1.0 + mod[1:2]) + mod[0:1]
    hb = h.astype(BF16)

    def tok(a, b):
        return _dot(hb, wtok_ref[:, a:b])

    def trn(a, b):
        return _nt_dot(wtrn_ref[a:b, :], hb)

    ct = ctok_ref[...]
    st = stok_ref[...]
    ka_ref[0] = (tok(_AK, _AKS) * ct[:, 0:128] + tok(_AKS, _BK) * st[:, 0:128]).astype(BF16)
    qta_ref[0, 0] = ((trn(_AQ, _AQS) * ctrn_ref[0:256, :] + trn(_AQS, _AV) * strn_ref[0:256, :]) * sc_a).astype(BF16)
    vta_ref[0, 0] = trn(_AV, _BQ).astype(BF16)
    kb_ref[0] = (tok(_BK, _BKS) * ct[:, 128:384] + tok(_BKS, _CQ) * st[:, 128:384]).astype(BF16)
    qtb_ref[0, 0] = ((trn(_BQ, _BQS) * ctrn_ref[256:512, :] + trn(_BQS, _BV) * strn_ref[256:512, :]) * sc_b).astype(BF16)
    vtb_ref[0, 0] = trn(_BV, _NTRN).astype(BF16)
    cq = tok(_CQ, _CKV)
    cqn = (cq * lax.rsqrt(jnp.sum(cq * cq, axis=-1, keepdims=True) * (1.0 / MLA_Q_RANK) + EPS) * gq_ref[...]).astype(BF16)
    qtc = _nt_dot(wuq_ref[...], cqn) * ctrn_ref[512:1024, :] + _nt_dot(wuqs_ref[...], cqn) * strn_ref[512:1024, :]
    qtc_ref[0, 0] = (qtc * sc_c).astype(BF16)
    ckv = tok(_CKV, _KR)
    ckvn = (ckv * lax.rsqrt(jnp.mean(ckv * ckv, axis=-1, keepdims=True) + EPS) * gkv_ref[...]).astype(BF16)
    krr = (tok(_KR, _KRS) * ct[:, 384:512] + tok(_KRS, _DQKV) * st[:, 384:512]).astype(BF16)
    kc_ref[0] = (_dot(ckvn, wkn_ref[...]) + _dot(krr, epl_ref[...])).astype(BF16)
    vtc_ref[0, 0] = _nt_dot(wvt_ref[...], ckvn).astype(BF16)
    dqkv_ref[0] = tok(_DQKV, _DZ)
    dz_ref[0] = tok(_DZ, _DG)
    dg_ref[0] = tok(_DG, _NTOK)


def _in_call(xu, mod8, g1, lw, tabs, nct):
    B, T, D = xu.shape
    nt = T // TILE
    tm = TILE

    def full(a):
        nd = a.ndim
        return pl.BlockSpec(a.shape, lambda b, t: (0,) * nd)

    def tokspec(w):
        return pl.BlockSpec((1, tm, w), lambda b, t: (b, t, 0))

    def trnspec(r):
        return pl.BlockSpec((1, 1, r, tm), lambda b, t: (b, t, 0, 0))

    scales = (HEAD_DIM ** -0.5 * LOG2E, DIFF_QK_DIM ** -0.5 * LOG2E, (MLA_NOPE + MLA_ROPE) ** -0.5 * LOG2E)
    weights = (lw["wtok"], lw["wtrn"])
    tables = (tabs["ctok"], tabs["stok"], tabs["ctrn"], tabs["strn"])
    mla = (lw["gq"], lw["gkv"], lw["wuq"], lw["wuqs"], lw["wkn"], lw["epl"], lw["wvt"])
    in_specs = [
        tokspec(D),
        pl.BlockSpec((1, 6, D), lambda b, t: (jnp.where(t < nct, 0, b + 1), 0, 0)),
        full(g1), full(weights[0]), full(weights[1]),
        pl.BlockSpec((tm, 512), lambda b, t: (t, 0)),
        pl.BlockSpec((tm, 512), lambda b, t: (t, 0)),
        pl.BlockSpec((1024, tm), lambda b, t: (0, t)),
        pl.BlockSpec((1024, tm), lambda b, t: (0, t)),
    ] + [full(a) for a in mla]
    out_shape = (
        jax.ShapeDtypeStruct((B, T, 128), BF16), jax.ShapeDtypeStruct((B, nt, 256, tm), BF16),
        jax.ShapeDtypeStruct((B, nt, 128, tm), BF16),
        jax.ShapeDtypeStruct((B, T, 256), BF16), jax.ShapeDtypeStruct((B, nt, 256, tm), BF16),
        jax.ShapeDtypeStruct((B, nt, 256, tm), BF16),
        jax.ShapeDtypeStruct((B, T, 512), BF16), jax.ShapeDtypeStruct((B, nt, 512, tm), BF16),
        jax.ShapeDtypeStruct((B, nt, 256, tm), BF16),
        jax.ShapeDtypeStruct((B, T, GDN_QKV), F32), jax.ShapeDtypeStruct((B, T, 256), F32),
        jax.ShapeDtypeStruct((B, T, 128), F32),
    )
    out_specs = (
        tokspec(128), trnspec(256), trnspec(128),
        tokspec(256), trnspec(256), trnspec(256),
        tokspec(512), trnspec(512), trnspec(256),
        tokspec(GDN_QKV), tokspec(256), tokspec(128),
    )
    return pl.pallas_call(
        functools.partial(_in_kernel, scales),
        out_shape=out_shape, grid=(B, nt), in_specs=in_specs, out_specs=out_specs,
        compiler_params=_cparams(("arbitrary", "arbitrary")),
        name="in_proj",
    )(xu, mod8, g1, *weights, *tables, *mla)


def _swa_kernel(nt, q_ref, k0_ref, k1_ref, k2_ref, k3_ref, v0_ref, v1_ref, v2_ref, v3_ref, sink_ref,
                o_ref, yt_ref):
    t = pl.program_id(1)
    tq = TILE
    r = lax.broadcasted_iota(jnp.int32, (tq, tq), 0)
    q = lax.broadcasted_iota(jnp.int32, (tq, tq), 1)
    d = r - q
    zero = jnp.zeros((tq, tq), F32)
    neg = jnp.full((tq, tq), NEG, F32)

    def gate(cond):
        return jnp.where(cond, 0.0, NEG).astype(F32)

    bias = (
        None,
        jnp.where(d >= WINDOW, zero, neg) + gate(t >= 2),
        jnp.where(jnp.abs(d) <= WINDOW, zero, neg) + gate(t >= 1),
        jnp.where(d <= -WINDOW, zero, neg) + gate(jnp.logical_and(t >= 1, t <= nt - 2)),
    )
    krefs = (k0_ref, k1_ref, k2_ref, k3_ref)
    vrefs = (v0_ref, v1_ref, v2_ref, v3_ref)
    zq = jnp.zeros((HEAD_DIM, tq), BF16)
    group = SWA_HEADS // SWA_KV_HEADS
    for h in range(SWA_HEADS):
        g = h // group
        qh = q_ref[0, 0, HEAD_DIM * h:HEAD_DIM * (h + 1), :]
        wq = jnp.concatenate([qh, zq], axis=0) if g == 0 else jnp.concatenate([zq, qh], axis=0)
        sink = sink_ref[h:h + 1, :]
        s = []
        m = sink
        for j in range(4):
            sj = _dot(krefs[j][0], wq)
            if bias[j] is not None:
                sj = sj + bias[j]
            s.append(sj)
            m = jnp.maximum(m, jnp.max(sj, axis=0, keepdims=True))
        l = jnp.exp2(sink - m)
        acc = jnp.zeros((HEAD_DIM, tq), F32)
        for j in range(4):
            p = jnp.exp2(s[j] - m)
            l = l + jnp.sum(p, axis=0, keepdims=True)
            acc = acc + _dot(vrefs[j][0, 0, HEAD_DIM * g:HEAD_DIM * (g + 1), :], p.astype(BF16))
        yt_ref[HEAD_DIM * h:HEAD_DIM * (h + 1), :] = acc / l
    o_ref[0] = yt_ref[...].T.astype(BF16)


def _swa_call(qta, ka, vta, sink_b):
    B, nt, R, tm = qta.shape
    T = nt * tm

    def kspec(f):
        return pl.BlockSpec((1, tm, 128), lambda b, t: (b, f(t), 0))

    def vspec(f):
        return pl.BlockSpec((1, 1, 128, tm), lambda b, t: (b, f(t), 0, 0))

    fs = (lambda t: 0, lambda t: jnp.maximum(t - 1, 0), lambda t: t, lambda t: jnp.minimum(t + 1, nt - 1))
    return pl.pallas_call(
        functools.partial(_swa_kernel, nt),
        out_shape=jax.ShapeDtypeStruct((B, T, 256), BF16),
        grid=(B, nt),
        in_specs=[pl.BlockSpec((1, 1, R, tm), lambda b, t: (b, t, 0, 0))]
        + [kspec(f) for f in fs] + [vspec(f) for f in fs]
        + [pl.BlockSpec(sink_b.shape, lambda b, t: (0, 0))],
        out_specs=pl.BlockSpec((1, tm, 256), lambda b, t: (b, t, 0)),
        scratch_shapes=[pltpu.VMEM((256, tm), F32)],
        compiler_params=_cparams(("arbitrary", "arbitrary")),
        name="swa_attn",
    )(qta, ka, ka, ka, ka, vta, vta, vta, vta, sink_b)


def _flash_maps(n_maps, n_pairs, get_k, get_wq, get_v, s_ref, p_ref, a_ref, acc_ref):
    tq = TILE
    n_last = 2 * n_pairs
    ones = jnp.ones((SUM_ROWS, tq), BF16)

    def stage_scores(c, slot):
        cc = jnp.minimum(c, n_last)
        for j in range(n_maps):
            s_ref[slot, j] = _dot(get_k(cc, j), get_wq(j))

    def stage_softmax(slot, ms):
        new_m = []
        for j in range(n_maps):
            s = s_ref[slot, j]
            m_new = jnp.maximum(ms[j], jnp.max(s, axis=0, keepdims=True))
            p_ref[slot, j] = jnp.exp2(s - m_new).astype(BF16)
            a_ref[slot, j, 0:1, :] = jnp.exp2(ms[j] - m_new)
            new_m.append(m_new)
        return tuple(new_m)

    def stage_values(c, slot):
        for j in range(n_maps):
            v1 = jnp.concatenate([get_v(c, j), ones], axis=0)
            acc_ref[j] = a_ref[slot, j, 0:1, :] * acc_ref[j] + _dot(v1, p_ref[slot, j])

    acc_ref[...] = jnp.zeros(acc_ref.shape, F32)
    stage_scores(0, 0)
    stage_scores(1, 1)
    ms = stage_softmax(0, tuple(jnp.full((1, tq), NEG, F32) for _ in range(n_maps)))

    def body(i, ms):
        c = 2 * i + 1
        stage_scores(c + 1, 0)
        ms = stage_softmax(1, ms)
        stage_values(c - 1, 0)
        stage_scores(c + 2, 1)
        ms = stage_softmax(0, ms)
        stage_values(c, 1)
        return ms

    lax.fori_loop(0, n_pairs, body, ms)
    stage_values(n_last, 0)


def _diff_kernel(nct, nt, lam_init, q_ref, k_ref, v_ref, lq1_ref, lk1_ref, lq2_ref, lk2_ref, g_ref,
                 o_ref, qm_ref, s_ref, p_ref, a_ref, acc_ref, yt_ref):
    t = pl.program_id(1)
    tq = TILE
    n_maps = 2 * DIFF_HEADS
    rows = lax.broadcasted_iota(jnp.int32, (LANES, tq), 0)
    per = LANES // DIFF_QK_DIM
    for j in range(n_maps):
        grp, sub = j // per, j % per
        qg = q_ref[0, 0, LANES * grp:LANES * (grp + 1), :]
        keep = jnp.logical_and(rows >= DIFF_QK_DIM * sub, rows < DIFF_QK_DIM * (sub + 1))
        qm_ref[j] = jnp.where(keep, qg, jnp.zeros_like(qg))

    def get_k(c, j):
        grp = j // per
        return k_ref[0, pl.ds(pl.multiple_of(c * tq, tq), tq), LANES * grp:LANES * (grp + 1)]

    def get_wq(j):
        return qm_ref[j]

    def get_v(c, j):
        h = j // 2
        return v_ref[0, c, DIFF_V_DIM * h:DIFF_V_DIM * (h + 1), :]

    n_pairs = jnp.where(t < nct, 0, (nt - nct) // 2)
    _flash_maps(n_maps, n_pairs, get_k, get_wq, get_v, s_ref, p_ref, a_ref, acc_ref)

    lam =(jnp.exp(jnp.sum(lq1_ref[...] * lk1_ref[...], axis=-1, keepdims=True))
           - jnp.exp(jnp.sum(lq2_ref[...] * lk2_ref[...], axis=-1, keepdims=True)) + lam_init)
    for h in range(DIFF_HEADS):
        dv = DIFF_V_DIM
        o1 = acc_ref[2 * h, 0:dv, :] / acc_ref[2 * h, dv:dv + 1, :]
        o2 = acc_ref[2 * h + 1, 0:dv, :] / acc_ref[2 * h + 1, dv:dv + 1, :]
        o = o1 - lam * o2
        on = o * lax.rsqrt(jnp.mean(o * o, axis=0, keepdims=True) + EPS) * g_ref[...]
        yt_ref[DIFF_V_DIM * h:DIFF_V_DIM * (h + 1), :] = on * (1.0 - lam_init)
    o_ref[0] = yt_ref[...].T.astype(BF16)


def _diff_call(qtb, kb, vtb, lq1, lk1, lq2, lk2, g_b, lam_init, nct):
    B, nt, R, tm = qtb.shape
    T = nt * tm
    n_maps = 2 * DIFF_HEADS
    small = [lq1, lk1, lq2, lk2, g_b]
    return pl.pallas_call(
        functools.partial(_diff_kernel, nct, nt, lam_init),
        out_shape=jax.ShapeDtypeStruct((B, T, 256), BF16),
        grid=(B, nt),
        in_specs=[
            pl.BlockSpec((1, 1, R, tm), lambda b, t: (b, t, 0, 0)),
            pl.BlockSpec((1, T, 256), lambda b, t: (b, 0, 0)),
            pl.BlockSpec((1, nt, 256, tm), lambda b, t: (b, 0, 0, 0)),
        ] + [pl.BlockSpec(a.shape, lambda b, t: (0, 0)) for a in small],
        out_specs=pl.BlockSpec((1, tm, 256), lambda b, t: (b, t, 0)),
        scratch_shapes=[
            pltpu.VMEM((n_maps, LANES, tm), BF16),
            pltpu.VMEM((2, n_maps, tm, tm), F32), pltpu.VMEM((2, n_maps, tm, tm), BF16),
            pltpu.VMEM((2, n_maps, 8, tm), F32),
            pltpu.VMEM((n_maps, DIFF_V_DIM + SUM_ROWS, tm), F32),
            pltpu.VMEM((256, tm), F32),
        ],
        compiler_params=_cparams(("arbitrary", "arbitrary")),
        name="diff_attn",
    )(qtb, kb, vtb, *small)


def _mla_kernel(nct, nt, q_ref, k_ref, v_ref, o_ref, s_ref, p_ref, a_ref, acc_ref, yt_ref):
    t = pl.program_id(1)
    tq = TILE

    def get_k(c, j):
        return k_ref[0, pl.ds(pl.multiple_of(c * tq, tq), tq), LANES * j:LANES * (j + 1)]

    def get_wq(j):
        return q_ref[0, 0, LANES * j:LANES * (j + 1), :]

    def get_v(c, j):
        return v_ref[0, c, MLA_V * j:MLA_V * (j + 1), :]

    n_pairs = jnp.where(t < nct, 0, (nt - nct) // 2)
    _flash_maps(MLA_HEADS, n_pairs, get_k, get_wq, get_v, s_ref, p_ref, a_ref, acc_ref)
    for h in range(MLA_HEADS):
        yt_ref[MLA_V * h:MLA_V * (h + 1), :] = acc_ref[h, 0:MLA_V, :] / acc_ref[h, MLA_V:MLA_V + 1, :]
    o_ref[0] = yt_ref[...].T.astype(BF16)


def _mla_call(qtc, kc, vtc, nct):
    B, nt, R, tm = qtc.shape
    T = nt * tm
    return pl.pallas_call(
        functools.partial(_mla_kernel, nct, nt),
        out_shape=jax.ShapeDtypeStruct((B, T, 256), BF16),
        grid=(B, nt),
        in_specs=[
            pl.BlockSpec((1, 1, R, tm), lambda b, t: (b, t, 0, 0)),
            pl.BlockSpec((1, T, 512), lambda b, t: (b, 0, 0)),
            pl.BlockSpec((1, nt, 256, tm), lambda b, t: (b, 0, 0, 0)),
        ],
        out_specs=pl.BlockSpec((1, tm, 256), lambda b, t: (b, t, 0)),
        scratch_shapes=[
            pltpu.VMEM((2, MLA_HEADS, tm, tm), F32), pltpu.VMEM((2, MLA_HEADS, tm, tm), BF16),
            pltpu.VMEM((2, MLA_HEADS, 8, tm), F32),
            pltpu.VMEM((MLA_HEADS, MLA_V + SUM_ROWS, tm), F32),
            pltpu.VMEM((256, tm), F32),
        ],
        compiler_params=_cparams(("arbitrary", "arbitrary")),
        name="mla_attn",
    )(qtc, kc, vtc)


def _gdn_prep_kernel(nct, nt, x_ref, xp_ref, xn_ref, w_ref, alog_ref, dtb_ref, dg_ref,
                     q_ref, k_ref, v_ref, gb_ref, xe_ref):
    t = pl.program_id(1)
    tm = TILE
    pad = GDN_CONV // 2
    keep_prev = jnp.where(jnp.logical_or(t == 0, t == nct), 0.0, 1.0).astype(F32)
    keep_next = jnp.where(jnp.logical_or(t == nct - 1, t == nt - 1), 0.0, 1.0).astype(F32)
    xe_ref[0:8, :] = xp_ref[0] * keep_prev
    xe_ref[8:8 + tm, :] = x_ref[0]
    xe_ref[8 + tm:16 + tm, :] = xn_ref[0] * keep_next
    y = jnp.zeros((tm, GDN_QKV), F32)
    for j in range(GDN_CONV):
        y = y + w_ref[j:j + 1, :] * xe_ref[8 - pad + j:8 - pad + j + tm, :]
    y = _silu(y)
    hq = GDN_HEADS * GDN_DK
    ones = _group_ones(hq, GDN_DK)
    q = y[:, 0:hq]
    k = y[:, hq:2 * hq]
    q_ref[0] = q * lax.rsqrt(_split_dot(q * q, ones) + EPS) * (GDN_DK ** -0.5)
    k_ref[0] = k * lax.rsqrt(_split_dot(k * k, ones) + EPS)
    v_ref[0] = y[:, 2 * hq:]
    dg = dg_ref[0]
    lane = lax.broadcasted_iota(jnp.int32, dg.shape, 1)
    xs = dg + dtb_ref[...]
    softplus = jnp.maximum(xs, 0.0) + jnp.log1p(jnp.exp(-jnp.abs(xs)))
    gate = jnp.where(lane < 2 * GDN_HEADS, -jnp.exp(alog_ref[...]) * softplus, 0.0)
    shift = GDN_CHUNK.bit_length() - 1
    rr = lax.broadcasted_iota(jnp.int32, (tm, tm), 0)
    cc = lax.broadcasted_iota(jnp.int32, (tm, tm), 1)
    same = lax.shift_right_logical(rr, shift) == lax.shift_right_logical(cc, shift)
    lower = jnp.where(jnp.logical_and(same, cc <= rr), 1.0, 0.0).astype(BF16)
    upper = jnp.where(jnp.logical_and(same, cc >= rr), 1.0, 0.0).astype(BF16)
    gcum = jnp.where(lane < GDN_HEADS, _msplit_dot(lower, gate), _msplit_dot(upper, gate))
    gb_ref[0] = jnp.where(lane < 2 * GDN_HEADS, gcum, jax.nn.sigmoid(dg))


def _gdn_prep_call(dqkv, dg, conv_w8, alog128, dtb128, nct):
    B, T, W = dqkv.shape
    nt = T // TILE
    tm = TILE
    hb = tm // 8
    return pl.pallas_call(
        functools.partial(_gdn_prep_kernel, nct, nt),
        out_shape=(jax.ShapeDtypeStruct((B, T, 256), F32),) * 3 + (jax.ShapeDtypeStruct((B, T, 128), F32),),
        grid=(B, nt),
        in_specs=[
            pl.BlockSpec((1, tm, W), lambda b, t: (b, t, 0)),
            pl.BlockSpec((1, 8, W), lambda b, t: (b, jnp.maximum(t * hb - 1, 0), 0)),
            pl.BlockSpec((1, 8, W), lambda b, t: (b, jnp.minimum((t + 1) * hb, T // 8 - 1), 0)),
            pl.BlockSpec((8, W), lambda b, t: (0, 0)),
            pl.BlockSpec((1, 128), lambda b, t: (0, 0)),
            pl.BlockSpec((1, 128), lambda b, t: (0, 0)),
            pl.BlockSpec((1, tm, 128), lambda b, t: (b, t, 0)),
        ],
        out_specs=(pl.BlockSpec((1, tm, 256), lambda b, t: (b, t, 0)),) * 3
        + (pl.BlockSpec((1, tm, 128), lambda b, t: (b, t, 0)),),
        scratch_shapes=[pltpu.VMEM((tm + 16, W), F32)],
        compiler_params=_cparams(("arbitrary", "arbitrary")),
        name="gdn_prep",
    )(dqkv, dqkv, dqkv, conv_w8, alog128, dtb128, dg)


def _head_blocks(x, bd):
    return jnp.concatenate([x] * GDN_HEADS, axis=0) * bd


def _row_blocks(x):
    cs = GDN_CHUNK
    out = x[0:cs]
    for h in range(1, GDN_HEADS):
        out = out + x[cs * h:cs * (h + 1)]
    return out


def _gdn_chunk_kernel(q_ref, k_ref, v_ref, g_ref, e_ref, m_ref, *out_refs):
    cs = GDN_CHUNK
    w = GDN_HEADS * GDN_DK
    bdb = m_ref[0].astype(BF16)
    ones = jnp.ones((cs, cs), BF16)

    def cat_dot3(x, y):
        xh = x.astype(BF16)
        xl = (x - xh.astype(F32)).astype(BF16)
        yh = y.astype(BF16)
        yl = (y - yh.astype(F32)).astype(BF16)
        r = _dot(jnp.concatenate([xh, xl], axis=0), _head_blocks(yh, bdb))
        return r[0:cs] + r[cs:2 * cs] + _dot(xh, _head_blocks(yl, bdb))

    chains = []
    for c in range(GDN_CHUNKS_PER_STEP):
        rows = slice(cs * c, cs * (c + 1))
        q, k, v, gates = q_ref[0, rows, :], k_ref[0, rows, :], v_ref[0, rows, :], g_ref[0, rows, :]
        g1 = gates.astype(BF16)
        r1 = gates - g1.astype(F32)
        g2 = r1.astype(BF16)
        g3 = (r1 - g2.astype(F32)).astype(BF16)
        k_bd = _head_blocks(k.astype(BF16), bdb)
        for d in range(2):
            incl, strict = _row_blocks(m_ref[1 + 2 * d]), _row_blocks(m_ref[2 + 2 * d])
            eye = incl - strict
            e = e_ref[d]
            ge = _dot(g1, e) + _dot(g2, e) + _dot(g3, e)
            gcw = ge[:, 0:w]
            betaw = ge[:, w:2 * w]
            gcrow = _msplit_dot(ones, gcw * eye)
            decay = jnp.exp((gcw - gcrow) * incl) * incl
            last_row = cs - 1 if d == 0 else 0
            glw = gcw[last_row:last_row + 1, :]
            kb = k * betaw
            a = _nt_dot(kb.astype(BF16), k_bd) * decay * strict
            att = _nt_dot(q.astype(BF16), k_bd) * decay
            chains.append(dict(c=c, d=d, rows=rows, q=q, k=k, vb=v * betaw, kb=kb, gcw=gcw, glw=glw, att=att,
                               pw=a, tinv=eye - a))

    for _ in range(5):
        for ch in chains:
            ch["pw"] = cat_dot3(ch["pw"], ch["pw"])
        for ch in chains:
            ch["tinv"] = ch["tinv"] + cat_dot3(ch["tinv"], ch["pw"])

    for ch in chains:
        c, d, rows = ch["c"], ch["d"], ch["rows"]
        eg = jnp.exp(ch["gcw"])
        tcat = ch["tinv"].astype(BF16)
        u = _dot(tcat, _head_blocks(ch["vb"].astype(BF16), bdb))
        wmat = _dot(tcat, _head_blocks((ch["kb"] * eg).astype(BF16), bdb))
        wq_ref, u_ref, att_ref, kd_ref, eg_ref = out_refs[5 * d:5 * d + 5]
        wq_ref[0, c] = jnp.concatenate([wmat, ch["q"] * eg], axis=0).astype(BF16)
        u_ref[0, rows, :] = u
        att_ref[0, rows, :] = ch["att"].astype(BF16)
        kd_ref[0, rows, :] = (ch["k"] * jnp.exp(ch["glw"] - ch["gcw"])).astype(BF16)
        eg_ref[0, c] = jnp.broadcast_to(jnp.exp(ch["glw"]), (8, w))


def _gdn_consts():
    w = GDN_HEADS * GDN_DK
    cs = GDN_CHUNK
    assert GDN_DK == cs and GDN_DV == cs
    ch = jnp.arange(128)[:, None]
    col = jnp.arange(2 * w)[None, :]
    seg, hcol = col // w, (col % w) // GDN_DK
    es = []
    for goff in (0, GDN_HEADS):
        pick = jnp.where(seg == 0, goff + hcol, 2 * GDN_HEADS + goff + hcol)
        es.append((ch == pick).astype(BF16))
    r = jnp.arange(GDN_HEADS * cs)[:, None]
    c = jnp.arange(GDN_HEADS * cs)[None, :]
    same = (r // cs) == (c // cs)
    masks = [same, same & (r >= c), same & (r > c), same & (r <= c), same & (r < c)]
    return jnp.stack(es), jnp.stack([m.astype(F32) for m in masks])


def _gdn_chunk_call(q, k, v, gbc, e2, masks):
    B, T, W = q.shape
    cs = GDN_CHUNK
    nc = T // cs
    per = GDN_CHUNKS_PER_STEP
    assert nc % per == 0

    def tok(width):
        return pl.BlockSpec((1, per * cs, width), lambda b, s: (b, s, 0))

    def full(a):
        nd = a.ndim
        return pl.BlockSpec(a.shape, lambda b, s: (0,) * nd)

    one_dir_shapes = (
        jax.ShapeDtypeStruct((B, nc, 2 * cs, W), BF16), jax.ShapeDtypeStruct((B, T, W), F32),
        jax.ShapeDtypeStruct((B, T, W), BF16), jax.ShapeDtypeStruct((B, T, W), BF16),
        jax.ShapeDtypeStruct((B, nc, 8, W), F32))
    one_dir_specs = (
        pl.BlockSpec((1, per, 2 * cs, W), lambda b, s: (b, s, 0, 0)), tok(W), tok(W), tok(W),
        pl.BlockSpec((1, per, 8, W), lambda b, s: (b, s, 0, 0)))
    return pl.pallas_call(
        _gdn_chunk_kernel,
        out_shape=one_dir_shapes * 2,
        grid=(B, nc // per),
        in_specs=[tok(W), tok(W), tok(W), tok(128), full(e2), full(masks)],
        out_specs=one_dir_specs * 2,
        compiler_params=_cparams(("arbitrary", "arbitrary")),
        name="gdn_chunk",
    )(q, k, v, gbc, e2, masks)


def _gdn_scan_kernel(bd_ref, wqf_ref, uf_ref, atf_ref, kdf_ref, egf_ref, wqb_ref, ub_ref, atb_ref, kdb_ref, egb_ref,
                     of_ref, ob_ref, sf_ref, sb_ref):
    cs = GDN_CHUNK

    @pl.when(pl.program_id(1) == 0)
    def _():
        sf_ref[...] = jnp.zeros(sf_ref.shape, F32)
        sb_ref[...] = jnp.zeros(sb_ref.shape, F32)

    bd = bd_ref[0]
    dirs = ((wqf_ref, uf_ref, atf_ref, kdf_ref, egf_ref, of_ref, sf_ref),
            (wqb_ref, ub_ref, atb_ref, kdb_ref, egb_ref, ob_ref, sb_ref))
    states = [d[6][...] for d in dirs]
    ws = [_dot(d[0][0, 0], s.astype(BF16)) for d, s in zip(dirs, states)]
    v_new = [d[1][0] - x[0:cs] for d, x in zip(dirs, ws)]
    vnb = [x.astype(BF16) for x in v_new]
    intra = [_dot(d[2][0], _head_blocks(x, bd).astype(BF16)) for d, x in zip(dirs, v_new)]
    upd = [_tn_dot(d[3][0], x) for d, x in zip(dirs, vnb)]
    for d, s, x, y, z in zip(dirs, states, ws, intra, upd):
        d[5][0] = x[cs:2 * cs] + y
        d[6][...] = s * d[4][0, 0][0:1, :] + z * bd


def _gdn_scan_call(chunk_outs, masks, nct):
    wqf, uf = chunk_outs[0], chunk_outs[1]
    B, T, W = uf.shape
    cs = GDN_CHUNK
    nc = T // cs
    ncc = nct * (TILE // cs)

    def fwd(b, s):
        return s

    def bwd(b, s):
        return jnp.where(s < ncc, ncc - 1 - s, nc - 1 - (s - ncc))

    specs = [pl.BlockSpec((1,) + masks.shape[1:], lambda b, s: (0, 0, 0))]
    for f in (fwd, bwd):
        specs += [
            pl.BlockSpec((1, 1, 2 * cs, W), lambda b, s, f=f: (b, f(b, s), 0, 0)),
            pl.BlockSpec((1, cs, W), lambda b, s, f=f: (b, f(b, s), 0)),
            pl.BlockSpec((1, cs, W), lambda b, s, f=f: (b, f(b, s), 0)),
            pl.BlockSpec((1, cs, W), lambda b, s, f=f: (b, f(b, s), 0)),
            pl.BlockSpec((1, 1, 8, W), lambda b, s, f=f: (b, f(b, s), 0, 0)),
        ]
    return pl.pallas_call(
        _gdn_scan_kernel,
        out_shape=(jax.ShapeDtypeStruct((B, T, W), F32),) * 2,
        grid=(B, nc),
        in_specs=specs,
        out_specs=(pl.BlockSpec((1, cs, W), lambda b, s: (b, fwd(b, s), 0)),
                   pl.BlockSpec((1, cs, W), lambda b, s: (b, bwd(b, s), 0))),
        scratch_shapes=[pltpu.VMEM((W, W), F32)] * 2,
        compiler_params=_cparams(("arbitrary", "arbitrary")),
        name="gdn_scan",
    )(masks, *chunk_outs)


def _route(logits_t, bias_col):
    ne, tm = logits_t.shape
    per = ne // N_GROUPS
    neg_inf = -jnp.inf
    scores = jax.nn.sigmoid(logits_t)
    choice = scores + bias_col
    idx_m = lax.broadcasted_iota(jnp.int32, (per, tm), 0).astype(F32)
    groups, gscore = [], []
    for g in range(N_GROUPS):
        cg = choice[per * g:per * (g + 1), :]
        m1 = jnp.max(cg, axis=0, keepdims=True)
        i1 = jnp.min(jnp.where(cg == m1, idx_m, float(per)), axis=0, keepdims=True)
        m2 = jnp.max(jnp.where(idx_m == i1, neg_inf, cg), axis=0, keepdims=True)
        groups.append(cg)
        gscore.append(m1 + m2)
    gsel = [jnp.zeros((1, tm), F32) for _ in range(N_GROUPS)]
    for _ in range(TOPK_GROUPS):
        gm = functools.reduce(jnp.maximum, gscore)
        found = jnp.zeros((1, tm), F32)
        for g in range(N_GROUPS):
            hit = jnp.logical_and(gscore[g] == gm, found < 0.5)
            gsel[g] = jnp.where(hit, 1.0, gsel[g])
            gscore[g] = jnp.where(hit, neg_inf, gscore[g])
            found = jnp.where(hit, 1.0, found)
    mc = jnp.concatenate([jnp.where(gsel[g] > 0.5, groups[g], neg_inf) for g in range(N_GROUPS)], axis=0)
    idx_e = lax.broadcasted_iota(jnp.int32, (ne, tm), 0).astype(F32)
    esel = jnp.zeros((ne, tm), F32)
    for _ in range(TOP_K):
        em = jnp.max(mc, axis=0, keepdims=True)
        ei = jnp.min(jnp.where(mc == em, idx_e, float(ne)), axis=0, keepdims=True)
        hit = idx_e == ei
        esel = jnp.where(hit, 1.0, esel)
        mc = jnp.where(hit, neg_inf, mc)
    picked = scores * esel
    return picked / jnp.sum(picked, axis=0, keepdims=True) * ROUTED_SCALE


def _out_kernel(n_ctx_rows, ya_ref, yb_ref, yc_ref, of_ref, ob_ref, z_ref, x_ref, modc_ref, modl_ref, gd_ref, wo_ref,
                g2_ref, wr_ref, rb_ref, x1_ref, h2_ref, wt_ref):
    tm = x_ref.shape[1]
    row = lax.broadcasted_iota(jnp.int32, (tm, 1), 0) + pl.program_id(1) * tm
    is_ctx = row < n_ctx_rows
    modc, modl = modc_ref[0], modl_ref[0]

    def mod_row(i):
        return jnp.where(is_ctx, modc[i:i + 1], modl[i:i + 1])

    o = of_ref[0] + ob_ref[0]
    ones = _group_ones(GDN_HEADS * GDN_DV, GDN_DV)
    ms = _split_dot(o * o, ones) * (1.0 / GDN_DV)
    yd = o * lax.rsqrt(ms + EPS) * gd_ref[...] * _silu(z_ref[0])
    y = (_dot(ya_ref[0], wo_ref[0:256, :]) + _dot(yb_ref[0], wo_ref[256:512, :])
         + _dot(yc_ref[0], wo_ref[512:768, :]) + _dot(yd.astype(BF16), wo_ref[768:1024, :]))
    x1 = x_ref[0] + mod_row(2) * y
    x1_ref[0] = x1
    h2 = x1 * lax.rsqrt(jnp.mean(x1 * x1, axis=-1, keepdims=True) + EPS) * g2_ref[...]
    h2 = h2 * (1.0 + mod_row(4)) + mod_row(3)
    h2_ref[0] = h2.astype(BF16)
    logits_t = _nt_dot(wr_ref[...], h2, precision=HIGHEST)
    w_t = _route(logits_t, rb_ref[...])
    w_pad = jnp.concatenate([w_t, jnp.zeros((LANES - N_EXPERTS, w_t.shape[1]), F32)], axis=0)
    wt_ref[0] = w_pad.T


def _token_tile(T, step, limit):
    best = step
    for cand in range(step, limit + 1, step):
        if T % cand == 0:
            best = cand
    return best


def _out_call(ya, yb, yc, of, ob, dz, xu, mod8, gd, wo, g2, wr_t, rbias, n_ctx_rows):
    B, T, D = xu.shape
    tm = _token_tile(T, TILE, 3 * TILE)

    def tokspec(w):
        return pl.BlockSpec((1, tm, w), lambda b, t: (b, t, 0))

    def full(a):
        nd = a.ndim
        return pl.BlockSpec(a.shape, lambda b, t: (0,) * nd)

    return pl.pallas_call(
        functools.partial(_out_kernel, n_ctx_rows),
        out_shape=(jax.ShapeDtypeStruct((B, T, D), F32), jax.ShapeDtypeStruct((B, T, D), BF16),
                   jax.ShapeDtypeStruct((B, T, 128), F32)),
        grid=(B, T // tm),
        in_specs=[tokspec(256)] * 6 + [
            tokspec(D),
            pl.BlockSpec((1, 6, D), lambda b, t: (0, 0, 0)),
            pl.BlockSpec((1, 6, D), lambda b, t: (b + 1, 0, 0)),
            full(gd), full(wo), full(g2), full(wr_t), full(rbias),
        ],
        out_specs=(tokspec(D), tokspec(D), tokspec(128)),
        compiler_params=_cparams(("arbitrary", "arbitrary")),
        name="out_proj_route",
    )(ya, yb, yc, of, ob, dz, xu, mod8, mod8, gd, wo, g2, wr_t, rbias)


def _moe_kernel(last, n_ctx_rows, x1_ref, h_ref, wt_ref, modc_ref, modl_ref, wg_ref, wu_ref, wd_ref,
                sg_ref, su_ref, sd_ref, gf_ref, o_ref, acc_ref):
    i = pl.program_id(1)
    g = pl.program_id(2)
    tm = h_ref.shape[1]
    n_groups = N_EXPERTS // MOE_EXPERTS_PER_STEP

    def swiglu(x, w_gate, w_up):
        return _silu(_dot(x, w_gate.astype(BF16))) * _dot(x, w_up.astype(BF16))

    @pl.when(g == 0)
    def _():
        acc_ref[...] = jnp.zeros(acc_ref.shape, F32)

    @pl.when(g < n_groups)
    def _():
        x = h_ref[0]
        wt = wt_ref[0]
        lane = lax.broadcasted_iota(jnp.int32, wt.shape, 1)
        hidden = []
        for j in range(MOE_EXPERTS_PER_STEP):
            e = g * MOE_EXPERTS_PER_STEP + j
            col = jnp.sum(jnp.where(lane == e, wt, 0.0), axis=-1, keepdims=True)
            hidden.append((swiglu(x, wg_ref[0, j], wu_ref[0, j]) * col).astype(BF16))
        w_down = wd_ref[0].astype(BF16)
        acc_ref[...] += _dot(jnp.concatenate(hidden, axis=-1), w_down.reshape(-1, w_down.shape[-1]))

    @pl.when(g == n_groups)
    def _():
        x = h_ref[0]
        f = acc_ref[...] + _dot(swiglu(x, sg_ref[0], su_ref[0]).astype(BF16), sd_ref[0].astype(BF16))
        row = lax.broadcasted_iota(jnp.int32, (tm, 1), 0) + i * tm
        gate = jnp.where(row < n_ctx_rows, modc_ref[0][5:6], modl_ref[0][5:6])
        x2 = x1_ref[0] + gate * f
        if last:
            x2 = x2 * lax.rsqrt(jnp.mean(x2 * x2, axis=-1, keepdims=True) + EPS) * gf_ref[...]
        o_ref[0] = x2


def _moe_call(x1, h2, wt, mod8, layer, wg, wu, wd, sg, su, sd, gfin, last, n_ctx_rows):
    B, T, D = x1.shape
    tm = _token_tile(T, LANES, MOE_TILE_LIMIT)
    F = wd.shape[-2]
    per = MOE_EXPERTS_PER_STEP
    n_groups = N_EXPERTS // per

    def tokspec(w):
        return pl.BlockSpec((1, tm, w), lambda b, i, g: (b, i, 0))

    def espec(shape):
        return pl.BlockSpec((1, per) + shape, lambda b, i, g: (layer, jnp.minimum(g, n_groups - 1), 0, 0))

    def sspec(shape):
        return pl.BlockSpec((1,) + shape, lambda b, i, g: (layer, 0, 0))

    return pl.pallas_call(
        functools.partial(_moe_kernel, last, n_ctx_rows),
        out_shape=jax.ShapeDtypeStruct((B, T, D), F32),
        grid=(B, T // tm, n_groups + 1),
        in_specs=[
            tokspec(D), tokspec(D), tokspec(128),
            pl.BlockSpec((1, 6, D), lambda b, i, g: (0, 0, 0)),
            pl.BlockSpec((1, 6, D), lambda b, i, g: (b + 1, 0, 0)),
            espec((D, F)), espec((D, F)), espec((F, D)),
            sspec((D, F)), sspec((D, F)), sspec((F, D)),
            pl.BlockSpec(gfin.shape, lambda b, i, g: (0, 0)),
        ],
        out_specs=tokspec(D),
        scratch_shapes=[pltpu.VMEM((tm, D), F32)],
        compiler_params=_cparams(("arbitrary", "arbitrary", "arbitrary")),
        name="moe_dense",
    )(x1, h2, wt, mod8, mod8, wg, wu, wd, sg, su, sd, gfin)


def _swap_cols(w, dim):
    d0, n = w.shape[0], w.shape[1] // dim
    w4 = w.reshape(d0, n, 2, dim // 2)
    return jnp.concatenate([-w4[:, :, 1:], w4[:, :, :1]], axis=2).reshape(d0, n * dim)


def _pad_cols(w, width):
    return jnp.pad(w, ((0, 0), (0, width - w.shape[1])))


def _layer_weights(w_in, gq, gkv, w_uq, w_ukv):
    cuts, acc = [], 0
    for s in IN_SPLITS[:-1]:
        acc += s
        cuts.append(acc)
    (aq, ak, av, bq, bk, bv, cq, ckv, kr, dqkv, dz, daf, dab, dbf, dbb) = jnp.split(w_in, cuts, axis=-1)
    wtok = jnp.concatenate([
        ak, _swap_cols(ak, HEAD_DIM), bk, _swap_cols(bk, DIFF_QK_DIM),
        _pad_cols(cq, 256), ckv, _pad_cols(kr, 128), _pad_cols(_swap_cols(kr, MLA_ROPE), 128),
        dqkv, dz, _pad_cols(jnp.concatenate([daf, dab, dbf, dbb], axis=-1), 128),
    ], axis=-1).astype(BF16)
    wtrn = jnp.concatenate([
        aq, _swap_cols(aq, HEAD_DIM), av, bq, _swap_cols(bq, DIFF_QK_DIM), bv,
    ], axis=-1).T.astype(BF16)
    qd = MLA_NOPE + MLA_ROPE
    uq = w_uq.reshape(MLA_Q_RANK, MLA_HEADS, qd)
    uq_n, uq_r = uq[:, :, :MLA_NOPE], uq[:, :, MLA_NOPE:]
    uq_rs = jnp.concatenate([-uq_r[:, :, MLA_ROPE // 2:], uq_r[:, :, :MLA_ROPE // 2]], axis=-1)
    zpad = jnp.zeros((MLA_Q_RANK, MLA_HEADS, LANES - qd), F32)
    wuq = jnp.concatenate([uq_n, uq_r, zpad], axis=-1).reshape(MLA_Q_RANK, MLA_HEADS * LANES)
    wuqs = jnp.concatenate([jnp.zeros_like(uq_n), uq_rs, zpad], axis=-1).reshape(MLA_Q_RANK, MLA_HEADS * LANES)
    wuq = jnp.pad(wuq.T, ((0, 0), (0, 256 - MLA_Q_RANK))).astype(BF16)
    wuqs = jnp.pad(wuqs.T, ((0, 0), (0, 256 - MLA_Q_RANK))).astype(BF16)
    ukv = w_ukv.reshape(MLA_KV_RANK, MLA_HEADS, MLA_NOPE + MLA_V)
    wkn = jnp.concatenate([ukv[:, :, :MLA_NOPE], jnp.zeros((MLA_KV_RANK, MLA_HEADS, LANES - MLA_NOPE), F32)],
                          axis=-1).reshape(MLA_KV_RANK, MLA_HEADS * LANES).astype(BF16)
    wvt = ukv[:, :, MLA_NOPE:].reshape(MLA_KV_RANK, MLA_HEADS * MLA_V).T.astype(BF16)
    rr = jnp.arange(LANES)[:, None]
    cc = jnp.arange(MLA_HEADS * LANES)[None, :]
    epl = jnp.where(jnp.logical_and(rr < MLA_ROPE, (cc % LANES) == rr + MLA_NOPE), 1.0, 0.0).astype(BF16)
    return dict(wtok=wtok, wtrn=wtrn, gq=_pad_cols(gq[None, :], 256), gkv=gkv[None, :],
                wuq=wuq, wuqs=wuqs, wkn=wkn, epl=epl, wvt=wvt)


def _rope_full(S, C, dim, feature_major):
    quarter = dim // 4
    inv = ROPE_BASE ** (-jnp.arange(quarter, dtype=F32) / quarter)
    rows = S // GRID_W
    r = jnp.repeat(jnp.arange(rows, dtype=F32), GRID_W)
    col = jnp.tile(jnp.arange(GRID_W, dtype=F32), rows)
    if feature_major:
        ang = jnp.concatenate([inv[:, None] * r[None, :], inv[:, None] * col[None, :]], axis=0)
        cos = jnp.concatenate([jnp.ones((dim // 2, C), F32), jnp.cos(ang)], axis=1)
        sin = jnp.concatenate([jnp.zeros((dim // 2, C), F32), jnp.sin(ang)], axis=1)
        return jnp.concatenate([cos, cos], axis=0), jnp.concatenate([sin, sin], axis=0)
    ang = jnp.concatenate([r[:, None] * inv, col[:, None] * inv], axis=-1)
    cos = jnp.concatenate([jnp.ones((C, dim // 2), F32), jnp.cos(ang)], axis=0)
    sin = jnp.concatenate([jnp.zeros((C, dim // 2), F32), jnp.sin(ang)], axis=0)
    return jnp.concatenate([cos, cos], axis=-1), jnp.concatenate([sin, sin], axis=-1)


def _rope_tables(S, C):
    T = S + C
    ca, sa = _rope_full(S, C, HEAD_DIM, False)
    cb, sb = _rope_full(S, C, DIFF_QK_DIM, False)
    cr, sr = _rope_full(S, C, MLA_ROPE, False)
    ctok = jnp.concatenate([jnp.tile(ca, (1, 2)), jnp.tile(cb, (1, 8)), _pad_cols(cr, 128)], axis=-1)
    stok = jnp.concatenate([jnp.tile(sa, (1, 2)), jnp.tile(sb, (1, 8)), _pad_cols(sr, 128)], axis=-1)
    cat, sat = _rope_full(S, C, HEAD_DIM, True)
    cbt, sbt = _rope_full(S, C, DIFF_QK_DIM, True)
    crt, srt = _rope_full(S, C, MLA_ROPE, True)
    one = jnp.ones((MLA_NOPE, T), F32)
    zero = jnp.zeros((MLA_NOPE, T), F32)
    padz = jnp.zeros((LANES - MLA_NOPE - MLA_ROPE, T), F32)
    cmla = jnp.tile(jnp.concatenate([one, crt, padz], axis=0), (MLA_HEADS, 1))
    smla = jnp.tile(jnp.concatenate([zero, srt, padz], axis=0), (MLA_HEADS, 1))
    ctrn = jnp.concatenate([jnp.tile(cat, (4, 1)), jnp.tile(cbt, (8, 1)), cmla], axis=0)
    strn = jnp.concatenate([jnp.tile(sat, (4, 1)), jnp.tile(sbt, (8, 1)), smla], axis=0)
    return dict(ctok=ctok, stok=stok, ctrn=ctrn, strn=strn)


def kernel(x, c, ctx, c_ctx, w_ada, b_ada, norm1_g, norm2_g, w_in, w_out, swa_sinks, diff_lq1, diff_lk1, diff_lq2, diff_lk2, diff_norm_g, mla_q_norm_g, mla_kv_norm_g, mla_w_uq, mla_w_ukv, gdn_conv_w, gdn_a_log_f, gdn_a_log_b, gdn_dt_bias_f, gdn_dt_bias_b, gdn_norm_g, moe_w_router, moe_bias, moe_w_gate, moe_w_up, moe_w_down, shared_w_gate, shared_w_up, shared_w_down, final_norm_g):
    B, S, D = x.shape
    C = ctx.shape[1]
    L = w_ada.shape[0]
    assert S % (2 * TILE) == 0 and C == TILE and B + 1 <= 8 and S % GRID_W == 0
    T = S + C
    nct = C // TILE

    xu = jnp.concatenate([ctx, x], axis=1)
    s8 = jnp.zeros((8, D), F32).at[0].set(c_ctx).at[1:B + 1].set(c)
    mod_all = _ada_call(s8, w_ada, b_ada).reshape(L, 8, 6, D)
    tabs = _rope_tables(S, C)
    gdn_e, gdn_masks = _gdn_consts()

    for l in range(L):
        last = l == L - 1
        lam_init = 0.8 - 0.6 * math.exp(-0.3 * l)
        mod8 = mod_all[l]
        lw = _layer_weights(w_in[l], mla_q_norm_g[l], mla_kv_norm_g[l], mla_w_uq[l], mla_w_ukv[l])
        (ka, qta, vta, kb, qtb, vtb, kc, qtc, vtc, dqkv, dz, dg) = _in_call(
            xu, mod8, norm1_g[l][None, :], lw, tabs, nct)

        sink_b = jnp.broadcast_to(jnp.pad(swa_sinks[l] * LOG2E, (0, 8 - SWA_HEADS))[:, None], (8, TILE))
        ya = _swa_call(qta, ka, vta, sink_b)
        g_b = jnp.broadcast_to(diff_norm_g[l][:, None], (DIFF_V_DIM, TILE))
        yb = _diff_call(qtb, kb, vtb, diff_lq1[l][None, :], diff_lk1[l][None, :], diff_lq2[l][None, :],
                        diff_lk2[l][None, :], g_b, lam_init, nct)
        yc = _mla_call(qtc, kc, vtc, nct)

        conv_w8 = jnp.pad(gdn_conv_w[l], ((0, 8 - GDN_CONV), (0, 0)))
        alog128 = jnp.pad(jnp.concatenate([gdn_a_log_f[l], gdn_a_log_b[l]]), (0, 128 - 2 * GDN_HEADS))[None, :]
        dtb128 = jnp.pad(jnp.concatenate([gdn_dt_bias_f[l], gdn_dt_bias_b[l]]), (0, 128 - 2 * GDN_HEADS))[None, :]
        gq, gk, gv, gb = _gdn_prep_call(dqkv, dg, conv_w8, alog128, dtb128, nct)
        of, ob = _gdn_scan_call(_gdn_chunk_call(gq, gk, gv, gb, gdn_e, gdn_masks), gdn_masks, nct)

        gd = jnp.tile(gdn_norm_g[l], GDN_HEADS)[None, :]
        x1, h2, wt = _out_call(ya, yb, yc, of, ob, dz, xu, mod8, gd, w_out[l].astype(BF16), norm2_g[l][None, :],
                               moe_w_router[l].T, moe_bias[l][:, None], C)
        xu = _moe_call(x1, h2, wt, mod8, l, moe_w_gate, moe_w_up, moe_w_down, shared_w_gate, shared_w_up,
                       shared_w_down, final_norm_g[None, :], last, C)
    return xu[:, C:, :]
```

```python
import functools
import math

import jax
import jax.numpy as jnp
from jax import lax
from jax.experimental import pallas as pl
from jax.experimental.pallas import tpu as pltpu

F32 = jnp.float32
BF16 = jnp.bfloat16
HIGHEST = lax.Precision.HIGHEST

GRID_W = 64
EPS = 1e-6
ROPE_BASE = 10000.0

HEAD_DIM = 64
SWA_HEADS = 4
SWA_KV_HEADS = 2
WINDOW = 128
DIFF_HEADS = 4
DIFF_QK_DIM = 32
DIFF_V_DIM = 64
MLA_HEADS = 4
MLA_Q_RANK = 192
MLA_KV_RANK = 128
MLA_NOPE = 64
MLA_ROPE = 32
MLA_V = 64
GDN_HEADS = 4
GDN_DK = 64
GDN_DV = 64
GDN_CONV = 5
GDN_CHUNK = 64
GDN_QKV = 2 * GDN_HEADS * GDN_DK + GDN_HEADS * GDN_DV
GDN_CHUNKS_PER_STEP = 4
N_EXPERTS = 64
N_GROUPS = 8
TOPK_GROUPS = 4
TOP_K = 6
EXPERT_FF = 256
ROUTED_SCALE = 2.5

IN_SPLITS = (
    SWA_HEADS * HEAD_DIM, SWA_KV_HEADS * HEAD_DIM, SWA_KV_HEADS * HEAD_DIM,
    DIFF_HEADS * 2 * DIFF_QK_DIM, DIFF_HEADS * 2 * DIFF_QK_DIM, DIFF_HEADS * DIFF_V_DIM,
    MLA_Q_RANK, MLA_KV_RANK, MLA_ROPE,
    GDN_QKV, GDN_HEADS * GDN_DV, GDN_HEADS, GDN_HEADS, GDN_HEADS, GDN_HEADS,
)

TILE = 256
LANES = 128
NEG = -1e30
SUM_ROWS = 16
LOG2E = math.log2(math.e)
VMEM_LIMIT = 56 * 1024 * 1024
MOE_TILE_LIMIT = 768
MOE_EXPERTS_PER_STEP = 4

_AK, _AKS, _BK, _BKS, _CQ, _CKV, _KR, _KRS, _DQKV, _DZ, _DG, _NTOK = (
    0, 128, 256, 512, 768, 1024, 1152, 1280, 1408, 2176, 2432, 2560)
_AQ, _AQS, _AV, _BQ, _BQS, _BV, _NTRN = 0, 256, 512, 640, 896, 1152, 1408


def _cparams(sem):
    return pltpu.CompilerParams(dimension_semantics=sem, vmem_limit_bytes=VMEM_LIMIT)


def _nt_dot(a, b, precision=None):
    return lax.dot_general(a, b, (((1,), (1,)), ((), ())), preferred_element_type=F32, precision=precision)


def _tn_dot(a, b):
    return lax.dot_general(a, b, (((0,), (0,)), ((), ())), preferred_element_type=F32)


def _dot(a, b, precision=None):
    return jnp.dot(a, b, preferred_element_type=F32, precision=precision)


def _split_dot(x, m):
    hi = x.astype(BF16)
    lo = (x - hi.astype(F32)).astype(BF16)
    return _dot(hi, m) + _dot(lo, m)


def _msplit_dot(m, x):
    x1 = x.astype(BF16)
    r1 = x - x1.astype(F32)
    x2 = r1.astype(BF16)
    x3 = (r1 - x2.astype(F32)).astype(BF16)
    return _dot(m, x1) + _dot(m, x2) + _dot(m, x3)


def _dot3(a, b):
    ah = a.astype(BF16)
    al = (a - ah.astype(F32)).astype(BF16)
    bh = b.astype(BF16)
    bl = (b - bh.astype(F32)).astype(BF16)
    return _dot(ah, bh) + _dot(ah, bl) + _dot(al, bh)


def _silu(x):
    return x * jax.nn.sigmoid(x)


def _group_ones(n, group):
    shift = group.bit_length() - 1
    r = lax.shift_right_logical(lax.broadcasted_iota(jnp.int32, (n, n), 0), shift)
    c = lax.shift_right_logical(lax.broadcasted_iota(jnp.int32, (n, n), 1), shift)
    return jnp.where(r == c, 1.0, 0.0).astype(BF16)


def _ada_kernel(s_ref, w_ref, b_ref, o_ref):
    s = _silu(s_ref[...])
    o_ref[0] = _dot(s, w_ref[0], precision=HIGHEST) + b_ref[0]


def _ada_call(s8, w_ada, b_ada):
    L, D, D6 = w_ada.shape
    nj = D6 // D
    return pl.pallas_call(
        _ada_kernel,
        out_shape=jax.ShapeDtypeStruct((L, 8, D6), F32),
        grid=(L, nj),
        in_specs=[
            pl.BlockSpec((8, D), lambda l, j: (0, 0)),
            pl.BlockSpec((1, D, D), lambda l, j: (l, 0, j)),
            pl.BlockSpec((1, 1, D), lambda l, j: (l, 0, j)),
        ],
        out_specs=pl.BlockSpec((1, 8, D), lambda l, j: (l, 0, j)),
        compiler_params=_cparams(("arbitrary", "arbitrary")),
        name="ada_mod",
    )(s8, w_ada, b_ada.reshape(L, 1, D6))


def _in_kernel(scales, x_ref, mod_ref, g1_ref, wtok_ref, wtrn_ref, ctok_ref, stok_ref, ctrn_ref, strn_ref,
               gq_ref, gkv_ref, wuq_ref, wuqs_ref, wkn_ref, epl_ref, wvt_ref,
               ka_ref, qta_ref, vta_ref, kb_ref, qtb_ref, vtb_ref, kc_ref, qtc_ref, vtc_ref,
               dqkv_ref, dz_ref, dg_ref):
    sc_a, sc_b, sc_c = scales
    x = x_ref[0]
    mod = mod_ref[0]
    h = x * lax.rsqrt(jnp.mean(x * x, axis=-1, keepdims=True) + EPS) * g1_ref[...]
    h = h * (1.0 + mod[1:2]) + mod[0:1]
    hb = h.astype(BF16)

    def tok(a, b):
        return _dot(hb, wtok_ref[:, a:b])

    def trn(a, b):
        return _nt_dot(wtrn_ref[a:b, :], hb)

    ct = ctok_ref[...]
    st = stok_ref[...]
    ka_ref[0] = (tok(_AK, _AKS) * ct[:, 0:128] + tok(_AKS, _BK) * st[:, 0:128]).astype(BF16)
    qta_ref[0, 0] = ((trn(_AQ, _AQS) * ctrn_ref[0:256, :] + trn(_AQS, _AV) * strn_ref[0:256, :]) * sc_a).astype(BF16)
    vta_ref[0, 0] = trn(_AV, _BQ).astype(BF16)
    kb_ref[0] = (tok(_BK, _BKS) * ct[:, 128:384] + tok(_BKS, _CQ) * st[:, 128:384]).astype(BF16)
    qtb_ref[0, 0] = ((trn(_BQ, _BQS) * ctrn_ref[256:512, :] + trn(_BQS, _BV) * strn_ref[256:512, :]) * sc_b).astype(BF16)
    vtb_ref[0, 0] = trn(_BV, _NTRN).astype(BF16)
    cq = tok(_CQ, _CKV)
    cqn = (cq * lax.rsqrt(jnp.sum(cq * cq, axis=-1, keepdims=True) * (1.0 / MLA_Q_RANK) + EPS) * gq_ref[...]).astype(BF16)
    qtc = _nt_dot(wuq_ref[...], cqn) * ctrn_ref[512:1024, :] + _nt_dot(wuqs_ref[...], cqn) * strn_ref[512:1024, :]
    qtc_ref[0, 0] = (qtc * sc_c).astype(BF16)
    ckv = tok(_CKV, _KR)
    ckvn = (ckv * lax.rsqrt(jnp.mean(ckv * ckv, axis=-1, keepdims=True) + EPS) * gkv_ref[...]).astype(BF16)
    krr = (tok(_KR, _KRS) * ct[:, 384:512] + tok(_KRS, _DQKV) * st[:, 384:512]).astype(BF16)
    kc_ref[0] = (_dot(ckvn, wkn_ref[...]) + _dot(krr, epl_ref[...])).astype(BF16)
    vtc_ref[0, 0] = _nt_dot(wvt_ref[...], ckvn).astype(BF16)
    dqkv_ref[0] = tok(_DQKV, _DZ)
    dz_ref[0] = tok(_DZ, _DG)
    dg_ref[0] = tok(_DG, _NTOK)


def _in_call(xu, mod8, g1, lw, tabs, nct):
    B, T, D = xu.shape
    nt = T // TILE
    tm = TILE

    def full(a):
        nd = a.ndim
        return pl.BlockSpec(a.shape, lambda b, t: (0,) * nd)

    def tokspec(w):
        return pl.BlockSpec((1, tm, w), lambda b, t: (b, t, 0))

    def trnspec(r):
        return pl.BlockSpec((1, 1, r, tm), lambda b, t: (b, t, 0, 0))

    scales = (HEAD_DIM ** -0.5 * LOG2E, DIFF_QK_DIM ** -0.5 * LOG2E, (MLA_NOPE + MLA_ROPE) ** -0.5 * LOG2E)
    weights = (lw["wtok"], lw["wtrn"])
    tables = (tabs["ctok"], tabs["stok"], tabs["ctrn"], tabs["strn"])
    mla = (lw["gq"], lw["gkv"], lw["wuq"], lw["wuqs"], lw["wkn"], lw["epl"], lw["wvt"])
    in_specs = [
        tokspec(D),
        pl.BlockSpec((1, 6, D), lambda b, t: (jnp.where(t < nct, 0, b + 1), 0, 0)),
        full(g1), full(weights[0]), full(weights[1]),
        pl.BlockSpec((tm, 512), lambda b, t: (t, 0)),
        pl.BlockSpec((tm, 512), lambda b, t: (t, 0)),
        pl.BlockSpec((1024, tm), lambda b, t: (0, t)),
        pl.BlockSpec((1024, tm), lambda b, t: (0, t)),
    ] + [full(a) for a in mla]
    out_shape = (
        jax.ShapeDtypeStruct((B, T, 128), BF16), jax.ShapeDtypeStruct((B, nt, 256, tm), BF16),
        jax.ShapeDtypeStruct((B, nt, 128, tm), BF16),
        jax.ShapeDtypeStruct((B, T, 256), BF16), jax.ShapeDtypeStruct((B, nt, 256, tm), BF16),
        jax.ShapeDtypeStruct((B, nt, 256, tm), BF16),
        jax.ShapeDtypeStruct((B, T, 512), BF16), jax.ShapeDtypeStruct((B, nt, 512, tm), BF16),
        jax.ShapeDtypeStruct((B, nt, 256, tm), BF16),
        jax.ShapeDtypeStruct((B, T, GDN_QKV), F32), jax.ShapeDtypeStruct((B, T, 256), F32),
        jax.ShapeDtypeStruct((B, T, 128), F32),
    )
    out_specs = (
        tokspec(128), trnspec(256), trnspec(128),
        tokspec(256), trnspec(256), trnspec(256),
        tokspec(512), trnspec(512), trnspec(256),
        tokspec(GDN_QKV), tokspec(256), tokspec(128),
    )
    return pl.pallas_call(
        functools.partial(_in_kernel, scales),
        out_shape=out_shape, grid=(B, nt), in_specs=in_specs, out_specs=out_specs,
        compiler_params=_cparams(("arbitrary", "arbitrary")),
        name="in_proj",
    )(xu, mod8, g1, *weights, *tables, *mla)


def _swa_kernel(nt, q_ref, k0_ref, k1_ref, k2_ref, k3_ref, v0_ref, v1_ref, v2_ref, v3_ref, sink_ref,
                o_ref, yt_ref):
    t = pl.program_id(1)
    tq = TILE
    r = lax.broadcasted_iota(jnp.int32, (tq, tq), 0)
    q = lax.broadcasted_iota(jnp.int32, (tq, tq), 1)
    d = r - q
    zero = jnp.zeros((tq, tq), F32)
    neg = jnp.full((tq, tq), NEG, F32)

    def gate(cond):
        return jnp.where(cond, 0.0, NEG).astype(F32)

    bias = (
        None,
        jnp.where(d >= WINDOW, zero, neg) + gate(t >= 2),
        jnp.where(jnp.abs(d) <= WINDOW, zero, neg) + gate(t >= 1),
        jnp.where(d <= -WINDOW, zero, neg) + gate(jnp.logical_and(t >= 1, t <= nt - 2)),
    )
    krefs = (k0_ref, k1_ref, k2_ref, k3_ref)
    vrefs = (v0_ref, v1_ref, v2_ref, v3_ref)
    zq = jnp.zeros((HEAD_DIM, tq), BF16)
    group = SWA_HEADS // SWA_KV_HEADS
    for h in range(SWA_HEADS):
        g = h // group
        qh = q_ref[0, 0, HEAD_DIM * h:HEAD_DIM * (h + 1), :]
        wq = jnp.concatenate([qh, zq], axis=0) if g == 0 else jnp.concatenate([zq, qh], axis=0)
        sink = sink_ref[h:h + 1, :]
        s = []
        m = sink
        for j in range(4):
            sj = _dot(krefs[j][0], wq)
            if bias[j] is not None:
                sj = sj + bias[j]
            s.append(sj)
            m = jnp.maximum(m, jnp.max(sj, axis=0, keepdims=True))
        l = jnp.exp2(sink - m)
        acc = jnp.zeros((HEAD_DIM, tq), F32)
        for j in range(4):
            p = jnp.exp2(s[j] - m)
            l = l + jnp.sum(p, axis=0, keepdims=True)
            acc = acc + _dot(vrefs[j][0, 0, HEAD_DIM * g:HEAD_DIM * (g + 1), :], p.astype(BF16))
        yt_ref[HEAD_DIM * h:HEAD_DIM * (h + 1), :] = acc / l
    o_ref[0] = yt_ref[...].T.astype(BF16)


def _swa_call(qta, ka, vta, sink_b):
    B, nt, R, tm = qta.shape
    T = nt * tm

    def kspec(f):
        return pl.BlockSpec((1, tm, 128), lambda b, t: (b, f(t), 0))

    def vspec(f):
        return pl.BlockSpec((1, 1, 128, tm), lambda b, t: (b, f(t), 0, 0))

    fs = (lambda t: 0, lambda t: jnp.maximum(t - 1, 0), lambda t: t, lambda t: jnp.minimum(t + 1, nt - 1))
    return pl.pallas_call(
        functools.partial(_swa_kernel, nt),
        out_shape=jax.ShapeDtypeStruct((B, T, 256), BF16),
        grid=(B, nt),
        in_specs=[pl.BlockSpec((1, 1, R, tm), lambda b, t: (b, t, 0, 0))]
        + [kspec(f) for f in fs] + [vspec(f) for f in fs]
        + [pl.BlockSpec(sink_b.shape, lambda b, t: (0, 0))],
        out_specs=pl.BlockSpec((1, tm, 256), lambda b, t: (b, t, 0)),
        scratch_shapes=[pltpu.VMEM((256, tm), F32)],
        compiler_params=_cparams(("arbitrary", "arbitrary")),
        name="swa_attn",
    )(qta, ka, ka, ka, ka, vta, vta, vta, vta, sink_b)


def _flash_maps(n_maps, n_pairs, get_k, get_wq, get_v, s_ref, p_ref, a_ref, acc_ref):
    tq = TILE
    n_last = 2 * n_pairs
    ones = jnp.ones((SUM_ROWS, tq), BF16)

    def stage_scores(c, slot):
        cc = jnp.minimum(c, n_last)
        for j in range(n_maps):
            s_ref[slot, j] = _dot(get_k(cc, j), get_wq(j))

    def stage_softmax(slot, ms):
        new_m = []
        for j in range(n_maps):
            s = s_ref[slot, j]
            m_new = jnp.maximum(ms[j], jnp.max(s, axis=0, keepdims=True))
            p_ref[slot, j] = jnp.exp2(s - m_new).astype(BF16)
            a_ref[slot, j, 0:1, :] = jnp.exp2(ms[j] - m_new)
            new_m.append(m_new)
        return tuple(new_m)

    def stage_values(c, slot):
        for j in range(n_maps):
            v1 = jnp.concatenate([get_v(c, j), ones], axis=0)
            acc_ref[j] = a_ref[slot, j, 0:1, :] * acc_ref[j] + _dot(v1, p_ref[slot, j])

    acc_ref[...] = jnp.zeros(acc_ref.shape, F32)
    stage_scores(0, 0)
    stage_scores(1, 1)
    ms = stage_softmax(0, tuple(jnp.full((1, tq), NEG, F32) for _ in range(n_maps)))

    def body(i, ms):
        c = 2 * i + 1
        stage_scores(c + 1, 0)
        ms = stage_softmax(1, ms)
        stage_values(c - 1, 0)
        stage_scores(c + 2, 1)
        ms = stage_softmax(0, ms)
        stage_values(c, 1)
        return ms

    lax.fori_loop(0, n_pairs, body, ms)
    stage_values(n_last, 0)


def _diff_kernel(nct, nt, lam_init, q_ref, k_ref, v_ref, lq1_ref, lk1_ref, lq2_ref, lk2_ref, g_ref,
                 o_ref, qm_ref, s_ref, p_ref, a_ref, acc_ref, yt_ref):
    t = pl.program_id(1)
    tq = TILE
    n_maps = 2 * DIFF_HEADS
    rows = lax.broadcasted_iota(jnp.int32, (LANES, tq), 0)
    per = LANES // DIFF_QK_DIM
    for j in range(n_maps):
        grp, sub = j // per, j % per
        qg = q_ref[0, 0, LANES * grp:LANES * (grp + 1), :]
        keep = jnp.logical_and(rows >= DIFF_QK_DIM * sub, rows < DIFF_QK_DIM * (sub + 1))
        qm_ref[j] = jnp.where(keep, qg, jnp.zeros_like(qg))

    def get_k(c, j):
        grp = j // per
        return k_ref[0, pl.ds(pl.multiple_of(c * tq, tq), tq), LANES * grp:LANES * (grp + 1)]

    def get_wq(j):
        return qm_ref[j]

    def get_v(c, j):
        h = j // 2
        return v_ref[0, c, DIFF_V_DIM * h:DIFF_V_DIM * (h + 1), :]

    n_pairs = jnp.where(t < nct, 0, (nt - nct) // 2)
    _flash_maps(n_maps, n_pairs, get_k, get_wq, get_v, s_ref, p_ref, a_ref, acc_ref)

    lam =(jnp.exp(jnp.sum(lq1_ref[...] * lk1_ref[...], axis=-1, keepdims=True))
           - jnp.exp(jnp.sum(lq2_ref[...] * lk2_ref[...], axis=-1, keepdims=True)) + lam_init)
    for h in range(DIFF_HEADS):
        dv = DIFF_V_DIM
        o1 = acc_ref[2 * h, 0:dv, :] / acc_ref[2 * h, dv:dv + 1, :]
        o2 = acc_ref[2 * h + 1, 0:dv, :] / acc_ref[2 * h + 1, dv:dv + 1, :]
        o = o1 - lam * o2
        on = o * lax.rsqrt(jnp.mean(o * o, axis=0, keepdims=True) + EPS) * g_ref[...]
        yt_ref[DIFF_V_DIM * h:DIFF_V_DIM * (h + 1), :] = on * (1.0 - lam_init)
    o_ref[0] = yt_ref[...].T.astype(BF16)


def _diff_call(qtb, kb, vtb, lq1, lk1, lq2, lk2, g_b, lam_init, nct):
    B, nt, R, tm = qtb.shape
    T = nt * tm
    n_maps = 2 * DIFF_HEADS
    small = [lq1, lk1, lq2, lk2, g_b]
    return pl.pallas_call(
        functools.partial(_diff_kernel, nct, nt, lam_init),
        out_shape=jax.ShapeDtypeStruct((B, T, 256), BF16),
        grid=(B, nt),
        in_specs=[
            pl.BlockSpec((1, 1, R, tm), lambda b, t: (b, t, 0, 0)),
            pl.BlockSpec((1, T, 256), lambda b, t: (b, 0, 0)),
            pl.BlockSpec((1, nt, 256, tm), lambda b, t: (b, 0, 0, 0)),
        ] + [pl.BlockSpec(a.shape, lambda b, t: (0, 0)) for a in small],
        out_specs=pl.BlockSpec((1, tm, 256), lambda b, t: (b, t, 0)),
        scratch_shapes=[
            pltpu.VMEM((n_maps, LANES, tm), BF16),
            pltpu.VMEM((2, n_maps, tm, tm), F32), pltpu.VMEM((2, n_maps, tm, tm), BF16),
            pltpu.VMEM((2, n_maps, 8, tm), F32),
            pltpu.VMEM((n_maps, DIFF_V_DIM + SUM_ROWS, tm), F32),
            pltpu.VMEM((256, tm), F32),
        ],
        compiler_params=_cparams(("arbitrary", "arbitrary")),
        name="diff_attn",
    )(qtb, kb, vtb, *small)


def _mla_kernel(nct, nt, q_ref, k_ref, v_ref, o_ref, s_ref, p_ref, a_ref, acc_ref, yt_ref):
    t = pl.program_id(1)
    tq = TILE

    def get_k(c, j):
        return k_ref[0, pl.ds(pl.multiple_of(c * tq, tq), tq), LANES * j:LANES * (j + 1)]

    def get_wq(j):
        return q_ref[0, 0, LANES * j:LANES * (j + 1), :]

    def get_v(c, j):
        return v_ref[0, c, MLA_V * j:MLA_V * (j + 1), :]

    n_pairs = jnp.where(t < nct, 0, (nt - nct) // 2)
    _flash_maps(MLA_HEADS, n_pairs, get_k, get_wq, get_v, s_ref, p_ref, a_ref, acc_ref)
    for h in range(MLA_HEADS):
        yt_ref[MLA_V * h:MLA_V * (h + 1), :] = acc_ref[h, 0:MLA_V, :] / acc_ref[h, MLA_V:MLA_V + 1, :]
    o_ref[0] = yt_ref[...].T.astype(BF16)


def _mla_call(qtc, kc, vtc, nct):
    B, nt, R, tm = qtc.shape
    T = nt * tm
    return pl.pallas_call(
        functools.partial(_mla_kernel, nct, nt),
        out_shape=jax.ShapeDtypeStruct((B, T, 256), BF16),
        grid=(B, nt),
        in_specs=[
            pl.BlockSpec((1, 1, R, tm), lambda b, t: (b, t, 0, 0)),
            pl.BlockSpec((1, T, 512), lambda b, t: (b, 0, 0)),
            pl.BlockSpec((1, nt, 256, tm), lambda b, t: (b, 0, 0, 0)),
        ],
        out_specs=pl.BlockSpec((1, tm, 256), lambda b, t: (b, t, 0)),
        scratch_shapes=[
            pltpu.VMEM((2, MLA_HEADS, tm, tm), F32), pltpu.VMEM((2, MLA_HEADS, tm, tm), BF16),
            pltpu.VMEM((2, MLA_HEADS, 8, tm), F32),
            pltpu.VMEM((MLA_HEADS, MLA_V + SUM_ROWS, tm), F32),
            pltpu.VMEM((256, tm), F32),
        ],
        compiler_params=_cparams(("arbitrary", "arbitrary")),
        name="mla_attn",
    )(qtc, kc, vtc)


def _gdn_prep_kernel(nct, nt, x_ref, xp_ref, xn_ref, w_ref, alog_ref, dtb_ref, dg_ref,
                     q_ref, k_ref, v_ref, gb_ref, xe_ref):
    t = pl.program_id(1)
    tm = TILE
    pad = GDN_CONV // 2
    keep_prev = jnp.where(jnp.logical_or(t == 0, t == nct), 0.0, 1.0).astype(F32)
    keep_next = jnp.where(jnp.logical_or(t == nct - 1, t == nt - 1), 0.0, 1.0).astype(F32)
    xe_ref[0:8, :] = xp_ref[0] * keep_prev
    xe_ref[8:8 + tm, :] = x_ref[0]
    xe_ref[8 + tm:16 + tm, :] = xn_ref[0] * keep_next
    y = jnp.zeros((tm, GDN_QKV), F32)
    for j in range(GDN_CONV):
        y = y + w_ref[j:j + 1, :] * xe_ref[8 - pad + j:8 - pad + j + tm, :]
    y = _silu(y)
    hq = GDN_HEADS * GDN_DK
    ones = _group_ones(hq, GDN_DK)
    q = y[:, 0:hq]
    k = y[:, hq:2 * hq]
    q_ref[0] = q * lax.rsqrt(_split_dot(q * q, ones) + EPS) * (GDN_DK ** -0.5)
    k_ref[0] = k * lax.rsqrt(_split_dot(k * k, ones) + EPS)
    v_ref[0] = y[:, 2 * hq:]
    dg = dg_ref[0]
    lane = lax.broadcasted_iota(jnp.int32, dg.shape, 1)
    xs = dg + dtb_ref[...]
    softplus = jnp.maximum(xs, 0.0) + jnp.log1p(jnp.exp(-jnp.abs(xs)))
    gate = jnp.where(lane < 2 * GDN_HEADS, -jnp.exp(alog_ref[...]) * softplus, 0.0)
    shift = GDN_CHUNK.bit_length() - 1
    rr = lax.broadcasted_iota(jnp.int32, (tm, tm), 0)
    cc = lax.broadcasted_iota(jnp.int32, (tm, tm), 1)
    same = lax.shift_right_logical(rr, shift) == lax.shift_right_logical(cc, shift)
    lower = jnp.where(jnp.logical_and(same, cc <= rr), 1.0, 0.0).astype(BF16)
    upper = jnp.where(jnp.logical_and(same, cc >= rr), 1.0, 0.0).astype(BF16)
    gcum = jnp.where(lane < GDN_HEADS, _msplit_dot(lower, gate), _msplit_dot(upper, gate))
    gb_ref[0] = jnp.where(lane < 2 * GDN_HEADS, gcum, jax.nn.sigmoid(dg))


def _gdn_prep_call(dqkv, dg, conv_w8, alog128, dtb128, nct):
    B, T, W = dqkv.shape
    nt = T // TILE
    tm = TILE
    hb = tm // 8
    return pl.pallas_call(
        functools.partial(_gdn_prep_kernel, nct, nt),
        out_shape=(jax.ShapeDtypeStruct((B, T, 256), F32),) * 3 + (jax.ShapeDtypeStruct((B, T, 128), F32),),
        grid=(B, nt),
        in_specs=[
            pl.BlockSpec((1, tm, W), lambda b, t: (b, t, 0)),
            pl.BlockSpec((1, 8, W), lambda b, t: (b, jnp.maximum(t * hb - 1, 0), 0)),
            pl.BlockSpec((1, 8, W), lambda b, t: (b, jnp.minimum((t + 1) * hb, T // 8 - 1), 0)),
            pl.BlockSpec((8, W), lambda b, t: (0, 0)),
            pl.BlockSpec((1, 128), lambda b, t: (0, 0)),
            pl.BlockSpec((1, 128), lambda b, t: (0, 0)),
            pl.BlockSpec((1, tm, 128), lambda b, t: (b, t, 0)),
        ],
        out_specs=(pl.BlockSpec((1, tm, 256), lambda b, t: (b, t, 0)),) * 3
        + (pl.BlockSpec((1, tm, 128), lambda b, t: (b, t, 0)),),
        scratch_shapes=[pltpu.VMEM((tm + 16, W), F32)],
        compiler_params=_cparams(("arbitrary", "arbitrary")),
        name="gdn_prep",
    )(dqkv, dqkv, dqkv, conv_w8, alog128, dtb128, dg)


def _head_blocks(x, bd):
    return jnp.concatenate([x] * GDN_HEADS, axis=0) * bd


def _row_blocks(x):
    cs = GDN_CHUNK
    out = x[0:cs]
    for h in range(1, GDN_HEADS):
        out = out + x[cs * h:cs * (h + 1)]
    return out


def _gdn_chunk_kernel(q_ref, k_ref, v_ref, g_ref, e_ref, m_ref, *out_refs):
    cs = GDN_CHUNK
    w = GDN_HEADS * GDN_DK
    bdb = m_ref[0].astype(BF16)
    ones = jnp.ones((cs, cs), BF16)

    def cat_dot3(x, y):
        xh = x.astype(BF16)
        xl = (x - xh.astype(F32)).astype(BF16)
        yh = y.astype(BF16)
        yl = (y - yh.astype(F32)).astype(BF16)
        r = _dot(jnp.concatenate([xh, xl], axis=0), _head_blocks(yh, bdb))
        return r[0:cs] + r[cs:2 * cs] + _dot(xh, _head_blocks(yl, bdb))

    chains = []
    for c in range(GDN_CHUNKS_PER_STEP):
        rows = slice(cs * c, cs * (c + 1))
        q, k, v, gates = q_ref[0, rows, :], k_ref[0, rows, :], v_ref[0, rows, :], g_ref[0, rows, :]
        g1 = gates.astype(BF16)
        r1 = gates - g1.astype(F32)
        g2 = r1.astype(BF16)
        g3 = (r1 - g2.astype(F32)).astype(BF16)
        k_bd = _head_blocks(k.astype(BF16), bdb)
        for d in range(2):
            incl, strict = _row_blocks(m_ref[1 + 2 * d]), _row_blocks(m_ref[2 + 2 * d])
            eye = incl - strict
            e = e_ref[d]
            ge = _dot(g1, e) + _dot(g2, e) + _dot(g3, e)
            gcw = ge[:, 0:w]
            betaw = ge[:, w:2 * w]
            gcrow = _msplit_dot(ones, gcw * eye)
            decay = jnp.exp((gcw - gcrow) * incl) * incl
            last_row = cs - 1 if d == 0 else 0
            glw = gcw[last_row:last_row + 1, :]
            kb = k * betaw
            a = _nt_dot(kb.astype(BF16), k_bd) * decay * strict
            att = _nt_dot(q.astype(BF16), k_bd) * decay
            chains.append(dict(c=c, d=d, rows=rows, q=q, k=k, vb=v * betaw, kb=kb, gcw=gcw, glw=glw, att=att,
                               pw=a, tinv=eye - a))

    for _ in range(5):
        for ch in chains:
            ch["pw"] = cat_dot3(ch["pw"], ch["pw"])
        for ch in chains:
            ch["tinv"] = ch["tinv"] + cat_dot3(ch["tinv"], ch["pw"])

    for ch in chains:
        c, d, rows = ch["c"], ch["d"], ch["rows"]
        eg = jnp.exp(ch["gcw"])
        tcat = ch["tinv"].astype(BF16)
        u = _dot(tcat, _head_blocks(ch["vb"].astype(BF16), bdb))
        wmat = _dot(tcat, _head_blocks((ch["kb"] * eg).astype(BF16), bdb))
        wq_ref, u_ref, att_ref, kd_ref, eg_ref = out_refs[5 * d:5 * d + 5]
        wq_ref[0, c] = jnp.concatenate([wmat, ch["q"] * eg], axis=0).astype(BF16)
        u_ref[0, rows, :] = u
        att_ref[0, rows, :] = ch["att"].astype(BF16)
        kd_ref[0, rows, :] = (ch["k"] * jnp.exp(ch["glw"] - ch["gcw"])).astype(BF16)
        eg_ref[0, c] = jnp.broadcast_to(jnp.exp(ch["glw"]), (8, w))


def _gdn_consts():
    w = GDN_HEADS * GDN_DK
    cs = GDN_CHUNK
    assert GDN_DK == cs and GDN_DV == cs
    ch = jnp.arange(128)[:, None]
    col = jnp.arange(2 * w)[None, :]
    seg, hcol = col // w, (col % w) // GDN_DK
    es = []
    for goff in (0, GDN_HEADS):
        pick = jnp.where(seg == 0, goff + hcol, 2 * GDN_HEADS + goff + hcol)
        es.append((ch == pick).astype(BF16))
    r = jnp.arange(GDN_HEADS * cs)[:, None]
    c = jnp.arange(GDN_HEADS * cs)[None, :]
    same = (r // cs) == (c // cs)
    masks = [same, same & (r >= c), same & (r > c), same & (r <= c), same & (r < c)]
    return jnp.stack(es), jnp.stack([m.astype(F32) for m in masks])


def _gdn_chunk_call(q, k, v, gbc, e2, masks):
    B, T, W = q.shape
    cs = GDN_CHUNK
    nc = T // cs
    per = GDN_CHUNKS_PER_STEP
    assert nc % per == 0

    def tok(width):
        return pl.BlockSpec((1, per * cs, width), lambda b, s: (b, s, 0))

    def full(a):
        nd = a.ndim
        return pl.BlockSpec(a.shape, lambda b, s: (0,) * nd)

    one_dir_shapes = (
        jax.ShapeDtypeStruct((B, nc, 2 * cs, W), BF16), jax.ShapeDtypeStruct((B, T, W), F32),
        jax.ShapeDtypeStruct((B, T, W), BF16), jax.ShapeDtypeStruct((B, T, W), BF16),
        jax.ShapeDtypeStruct((B, nc, 8, W), F32))
    one_dir_specs = (
        pl.BlockSpec((1, per, 2 * cs, W), lambda b, s: (b, s, 0, 0)), tok(W), tok(W), tok(W),
        pl.BlockSpec((1, per, 8, W), lambda b, s: (b, s, 0, 0)))
    return pl.pallas_call(
        _gdn_chunk_kernel,
        out_shape=one_dir_shapes * 2,
        grid=(B, nc // per),
        in_specs=[tok(W), tok(W), tok(W), tok(128), full(e2), full(masks)],
        out_specs=one_dir_specs * 2,
        compiler_params=_cparams(("arbitrary", "arbitrary")),
        name="gdn_chunk",
    )(q, k, v, gbc, e2, masks)


def _gdn_scan_kernel(bd_ref, wqf_ref, uf_ref, atf_ref, kdf_ref, egf_ref, wqb_ref, ub_ref, atb_ref, kdb_ref, egb_ref,
                     of_ref, ob_ref, sf_ref, sb_ref):
    cs = GDN_CHUNK

    @pl.when(pl.program_id(1) == 0)
    def _():
        sf_ref[...] = jnp.zeros(sf_ref.shape, F32)
        sb_ref[...] = jnp.zeros(sb_ref.shape, F32)

    bd = bd_ref[0]
    dirs = ((wqf_ref, uf_ref, atf_ref, kdf_ref, egf_ref, of_ref, sf_ref),
            (wqb_ref, ub_ref, atb_ref, kdb_ref, egb_ref, ob_ref, sb_ref))
    states = [d[6][...] for d in dirs]
    ws = [_dot(d[0][0, 0], s.astype(BF16)) for d, s in zip(dirs, states)]
    v_new = [d[1][0] - x[0:cs] for d, x in zip(dirs, ws)]
    vnb = [x.astype(BF16) for x in v_new]
    intra = [_dot(d[2][0], _head_blocks(x, bd).astype(BF16)) for d, x in zip(dirs, v_new)]
    upd = [_tn_dot(d[3][0], x) for d, x in zip(dirs, vnb)]
    for d, s, x, y, z in zip(dirs, states, ws, intra, upd):
        d[5][0] = x[cs:2 * cs] + y
        d[6][...] = s * d[4][0, 0][0:1, :] + z * bd


def _gdn_scan_call(chunk_outs, masks, nct):
    wqf, uf = chunk_outs[0], chunk_outs[1]
    B, T, W = uf.shape
    cs = GDN_CHUNK
    nc = T // cs
    ncc = nct * (TILE // cs)

    def fwd(b, s):
        return s

    def bwd(b, s):
        return jnp.where(s < ncc, ncc - 1 - s, nc - 1 - (s - ncc))

    specs = [pl.BlockSpec((1,) + masks.shape[1:], lambda b, s: (0, 0, 0))]
    for f in (fwd, bwd):
        specs += [
            pl.BlockSpec((1, 1, 2 * cs, W), lambda b, s, f=f: (b, f(b, s), 0, 0)),
            pl.BlockSpec((1, cs, W), lambda b, s, f=f: (b, f(b, s), 0)),
            pl.BlockSpec((1, cs, W), lambda b, s, f=f: (b, f(b, s), 0)),
            pl.BlockSpec((1, cs, W), lambda b, s, f=f: (b, f(b, s), 0)),
            pl.BlockSpec((1, 1, 8, W), lambda b, s, f=f: (b, f(b, s), 0, 0)),
        ]
    return pl.pallas_call(
        _gdn_scan_kernel,
        out_shape=(jax.ShapeDtypeStruct((B, T, W), F32),) * 2,
        grid=(B, nc),
        in_specs=specs,
        out_specs=(pl.BlockSpec((1, cs, W), lambda b, s: (b, fwd(b, s), 0)),
                   pl.BlockSpec((1, cs, W), lambda b, s: (b, bwd(b, s), 0))),
        scratch_shapes=[pltpu.VMEM((W, W), F32)] * 2,
        compiler_params=_cparams(("arbitrary", "arbitrary")),
        name="gdn_scan",
    )(masks, *chunk_outs)


def _route(logits_t, bias_col):
    ne, tm = logits_t.shape
    per = ne // N_GROUPS
    neg_inf = -jnp.inf
    scores = jax.nn.sigmoid(logits_t)
    choice = scores + bias_col
    idx_m = lax.broadcasted_iota(jnp.int32, (per, tm), 0).astype(F32)
    groups, gscore = [], []
    for g in range(N_GROUPS):
        cg = choice[per * g:per * (g + 1), :]
        m1 = jnp.max(cg, axis=0, keepdims=True)
        i1 = jnp.min(jnp.where(cg == m1, idx_m, float(per)), axis=0, keepdims=True)
        m2 = jnp.max(jnp.where(idx_m == i1, neg_inf, cg), axis=0, keepdims=True)
        groups.append(cg)
        gscore.append(m1 + m2)
    gsel = [jnp.zeros((1, tm), F32) for _ in range(N_GROUPS)]
    for _ in range(TOPK_GROUPS):
        gm = functools.reduce(jnp.maximum, gscore)
        found = jnp.zeros((1, tm), F32)
        for g in range(N_GROUPS):
            hit = jnp.logical_and(gscore[g] == gm, found < 0.5)
            gsel[g] = jnp.where(hit, 1.0, gsel[g])
            gscore[g] = jnp.where(hit, neg_inf, gscore[g])
            found = jnp.where(hit, 1.0, found)
    mc = jnp.concatenate([jnp.where(gsel[g] > 0.5, groups[g], neg_inf) for g in range(N_GROUPS)], axis=0)
    idx_e = lax.broadcasted_iota(jnp.int32, (ne, tm), 0).astype(F32)
    esel = jnp.zeros((ne, tm), F32)
    for _ in range(TOP_K):
        em = jnp.max(mc, axis=0, keepdims=True)
        ei = jnp.min(jnp.where(mc == em, idx_e, float(ne)), axis=0, keepdims=True)
        hit = idx_e == ei
        esel = jnp.where(hit, 1.0, esel)
        mc = jnp.where(hit, neg_inf, mc)
    picked = scores * esel
    return picked / jnp.sum(picked, axis=0, keepdims=True) * ROUTED_SCALE


def _out_kernel(n_ctx_rows, ya_ref, yb_ref, yc_ref, of_ref, ob_ref, z_ref, x_ref, modc_ref, modl_ref, gd_ref, wo_ref,
                g2_ref, wr_ref, rb_ref, x1_ref, h2_ref, wt_ref):
    tm = x_ref.shape[1]
    row = lax.broadcasted_iota(jnp.int32, (tm, 1), 0) + pl.program_id(1) * tm
    is_ctx = row < n_ctx_rows
    modc, modl = modc_ref[0], modl_ref[0]

    def mod_row(i):
        return jnp.where(is_ctx, modc[i:i + 1], modl[i:i + 1])

    o = of_ref[0] + ob_ref[0]
    ones = _group_ones(GDN_HEADS * GDN_DV, GDN_DV)
    ms = _split_dot(o * o, ones) * (1.0 / GDN_DV)
    yd = o * lax.rsqrt(ms + EPS) * gd_ref[...] * _silu(z_ref[0])
    y = (_dot(ya_ref[0], wo_ref[0:256, :]) + _dot(yb_ref[0], wo_ref[256:512, :])
         + _dot(yc_ref[0], wo_ref[512:768, :]) + _dot(yd.astype(BF16), wo_ref[768:1024, :]))
    x1 = x_ref[0] + mod_row(2) * y
    x1_ref[0] = x1
    h2 = x1 * lax.rsqrt(jnp.mean(x1 * x1, axis=-1, keepdims=True) + EPS) * g2_ref[...]
    h2 = h2 * (1.0 + mod_row(4)) + mod_row(3)
    h2_ref[0] = h2.astype(BF16)
    logits_t = _nt_dot(wr_ref[...], h2, precision=HIGHEST)
    w_t = _route(logits_t, rb_ref[...])
    w_pad = jnp.concatenate([w_t, jnp.zeros((LANES - N_EXPERTS, w_t.shape[1]), F32)], axis=0)
    wt_ref[0] = w_pad.T


def _token_tile(T, step, limit):
    best = step
    for cand in range(step, limit + 1, step):
        if T % cand == 0:
            best = cand
    return best


def _out_call(ya, yb, yc, of, ob, dz, xu, mod8, gd, wo, g2, wr_t, rbias, n_ctx_rows):
    B, T, D = xu.shape
    tm = _token_tile(T, TILE, 3 * TILE)

    def tokspec(w):
        return pl.BlockSpec((1, tm, w), lambda b, t: (b, t, 0))

    def full(a):
        nd = a.ndim
        return pl.BlockSpec(a.shape, lambda b, t: (0,) * nd)

    return pl.pallas_call(
        functools.partial(_out_kernel, n_ctx_rows),
        out_shape=(jax.ShapeDtypeStruct((B, T, D), F32), jax.ShapeDtypeStruct((B, T, D), BF16),
                   jax.ShapeDtypeStruct((B, T, 128), F32)),
        grid=(B, T // tm),
        in_specs=[tokspec(256)] * 6 + [
            tokspec(D),
            pl.BlockSpec((1, 6, D), lambda b, t: (0, 0, 0)),
            pl.BlockSpec((1, 6, D), lambda b, t: (b + 1, 0, 0)),
            full(gd), full(wo), full(g2), full(wr_t), full(rbias),
        ],
        out_specs=(tokspec(D), tokspec(D), tokspec(128)),
        compiler_params=_cparams(("arbitrary", "arbitrary")),
        name="out_proj_route",
    )(ya, yb, yc, of, ob, dz, xu, mod8, mod8, gd, wo, g2, wr_t, rbias)


def _moe_kernel(last, n_ctx_rows, x1_ref, h_ref, wt_ref, modc_ref, modl_ref, wg_ref, wu_ref, wd_ref,
                sg_ref, su_ref, sd_ref, gf_ref, o_ref, acc_ref):
    i = pl.program_id(1)
    g = pl.program_id(2)
    tm = h_ref.shape[1]
    n_groups = N_EXPERTS // MOE_EXPERTS_PER_STEP

    def swiglu(x, w_gate, w_up):
        return _silu(_dot(x, w_gate.astype(BF16))) * _dot(x, w_up.astype(BF16))

    @pl.when(g == 0)
    def _():
        acc_ref[...] = jnp.zeros(acc_ref.shape, F32)

    @pl.when(g < n_groups)
    def _():
        x = h_ref[0]
        wt = wt_ref[0]
        lane = lax.broadcasted_iota(jnp.int32, wt.shape, 1)
        hidden = []
        for j in range(MOE_EXPERTS_PER_STEP):
            e = g * MOE_EXPERTS_PER_STEP + j
            col = jnp.sum(jnp.where(lane == e, wt, 0.0), axis=-1, keepdims=True)
            hidden.append((swiglu(x, wg_ref[0, j], wu_ref[0, j]) * col).astype(BF16))
        w_down = wd_ref[0].astype(BF16)
        acc_ref[...] += _dot(jnp.concatenate(hidden, axis=-1), w_down.reshape(-1, w_down.shape[-1]))

    @pl.when(g == n_groups)
    def _():
        x = h_ref[0]
        f = acc_ref[...] + _dot(swiglu(x, sg_ref[0], su_ref[0]).astype(BF16), sd_ref[0].astype(BF16))
        row = lax.broadcasted_iota(jnp.int32, (tm, 1), 0) + i * tm
        gate = jnp.where(row < n_ctx_rows, modc_ref[0][5:6], modl_ref[0][5:6])
        x2 = x1_ref[0] + gate * f
        if last:
            x2 = x2 * lax.rsqrt(jnp.mean(x2 * x2, axis=-1, keepdims=True) + EPS) * gf_ref[...]
        o_ref[0] = x2


def _moe_call(x1, h2, wt, mod8, layer, wg, wu, wd, sg, su, sd, gfin, last, n_ctx_rows):
    B, T, D = x1.shape
    tm = _token_tile(T, LANES, MOE_TILE_LIMIT)
    F = wd.shape[-2]
    per = MOE_EXPERTS_PER_STEP
    n_groups = N_EXPERTS // per

    def tokspec(w):
        return pl.BlockSpec((1, tm, w), lambda b, i, g: (b, i, 0))

    def espec(shape):
        return pl.BlockSpec((1, per) + shape, lambda b, i, g: (layer, jnp.minimum(g, n_groups - 1), 0, 0))

    def sspec(shape):
        return pl.BlockSpec((1,) + shape, lambda b, i, g: (layer, 0, 0))

    return pl.pallas_call(
        functools.partial(_moe_kernel, last, n_ctx_rows),
        out_shape=jax.ShapeDtypeStruct((B, T, D), F32),
        grid=(B, T // tm, n_groups + 1),
        in_specs=[
            tokspec(D), tokspec(D), tokspec(128),
            pl.BlockSpec((1, 6, D), lambda b, i, g: (0, 0, 0)),
            pl.BlockSpec((1, 6, D), lambda b, i, g: (b + 1, 0, 0)),
            espec((D, F)), espec((D, F)), espec((F, D)),
            sspec((D, F)), sspec((D, F)), sspec((F, D)),
            pl.BlockSpec(gfin.shape, lambda b, i, g: (0, 0)),
        ],
        out_specs=tokspec(D),
        scratch_shapes=[pltpu.VMEM((tm, D), F32)],
        compiler_params=_cparams(("arbitrary", "arbitrary", "arbitrary")),
        name="moe_dense",
    )(x1, h2, wt, mod8, mod8, wg, wu, wd, sg, su, sd, gfin)


def _swap_cols(w, dim):
    d0, n = w.shape[0], w.shape[1] // dim
    w4 = w.reshape(d0, n, 2, dim // 2)
    return jnp.concatenate([-w4[:, :, 1:], w4[:, :, :1]], axis=2).reshape(d0, n * dim)


def _pad_cols(w, width):
    return jnp.pad(w, ((0, 0), (0, width - w.shape[1])))


def _layer_weights(w_in, gq, gkv, w_uq, w_ukv):
    cuts, acc = [], 0
    for s in IN_SPLITS[:-1]:
        acc += s
        cuts.append(acc)
    (aq, ak, av, bq, bk, bv, cq, ckv, kr, dqkv, dz, daf, dab, dbf, dbb) = jnp.split(w_in, cuts, axis=-1)
    wtok = jnp.concatenate([
        ak, _swap_cols(ak, HEAD_DIM), bk, _swap_cols(bk, DIFF_QK_DIM),
        _pad_cols(cq, 256), ckv, _pad_cols(kr, 128), _pad_cols(_swap_cols(kr, MLA_ROPE), 128),
        dqkv, dz, _pad_cols(jnp.concatenate([daf, dab, dbf, dbb], axis=-1), 128),
    ], axis=-1).astype(BF16)
    wtrn = jnp.concatenate([
        aq, _swap_cols(aq, HEAD_DIM), av, bq, _swap_cols(bq, DIFF_QK_DIM), bv,
    ], axis=-1).T.astype(BF16)
    qd = MLA_NOPE + MLA_ROPE
    uq = w_uq.reshape(MLA_Q_RANK, MLA_HEADS, qd)
    uq_n, uq_r = uq[:, :, :MLA_NOPE], uq[:, :, MLA_NOPE:]
    uq_rs = jnp.concatenate([-uq_r[:, :, MLA_ROPE // 2:], uq_r[:, :, :MLA_ROPE // 2]], axis=-1)
    zpad = jnp.zeros((MLA_Q_RANK, MLA_HEADS, LANES - qd), F32)
    wuq = jnp.concatenate([uq_n, uq_r, zpad], axis=-1).reshape(MLA_Q_RANK, MLA_HEADS * LANES)
    wuqs = jnp.concatenate([jnp.zeros_like(uq_n), uq_rs, zpad], axis=-1).reshape(MLA_Q_RANK, MLA_HEADS * LANES)
    wuq = jnp.pad(wuq.T, ((0, 0), (0, 256 - MLA_Q_RANK))).astype(BF16)
    wuqs = jnp.pad(wuqs.T, ((0, 0), (0, 256 - MLA_Q_RANK))).astype(BF16)
    ukv = w_ukv.reshape(MLA_KV_RANK, MLA_HEADS, MLA_NOPE + MLA_V)
    wkn = jnp.concatenate([ukv[:, :, :MLA_NOPE], jnp.zeros((MLA_KV_RANK, MLA_HEADS, LANES - MLA_NOPE), F32)],
                          axis=-1).reshape(MLA_KV_RANK, MLA_HEADS * LANES).astype(BF16)
    wvt = ukv[:, :, MLA_NOPE:].reshape(MLA_KV_RANK, MLA_HEADS * MLA_V).T.astype(BF16)
    rr = jnp.arange(LANES)[:, None]
    cc = jnp.arange(MLA_HEADS * LANES)[None, :]
    epl = jnp.where(jnp.logical_and(rr < MLA_ROPE, (cc % LANES) == rr + MLA_NOPE), 1.0, 0.0).astype(BF16)
    return dict(wtok=wtok, wtrn=wtrn, gq=_pad_cols(gq[None, :], 256), gkv=gkv[None, :],
                wuq=wuq, wuqs=wuqs, wkn=wkn, epl=epl, wvt=wvt)


def _rope_full(S, C, dim, feature_major):
    quarter = dim // 4
    inv = ROPE_BASE ** (-jnp.arange(quarter, dtype=F32) / quarter)
    rows = S // GRID_W
    r = jnp.repeat(jnp.arange(rows, dtype=F32), GRID_W)
    col = jnp.tile(jnp.arange(GRID_W, dtype=F32), rows)
    if feature_major:
        ang = jnp.concatenate([inv[:, None] * r[None, :], inv[:, None] * col[None, :]], axis=0)
        cos = jnp.concatenate([jnp.ones((dim // 2, C), F32), jnp.cos(ang)], axis=1)
        sin = jnp.concatenate([jnp.zeros((dim // 2, C), F32), jnp.sin(ang)], axis=1)
        return jnp.concatenate([cos, cos], axis=0), jnp.concatenate([sin, sin], axis=0)
    ang = jnp.concatenate([r[:, None] * inv, col[:, None] * inv], axis=-1)
    cos = jnp.concatenate([jnp.ones((C, dim // 2), F32), jnp.cos(ang)], axis=0)
    sin = jnp.concatenate([jnp.zeros((C, dim // 2), F32), jnp.sin(ang)], axis=0)
    return jnp.concatenate([cos, cos], axis=-1), jnp.concatenate([sin, sin], axis=-1)


def _rope_tables(S, C):
    T = S + C
    ca, sa = _rope_full(S, C, HEAD_DIM, False)
    cb, sb = _rope_full(S, C, DIFF_QK_DIM, False)
    cr, sr = _rope_full(S, C, MLA_ROPE, False)
    ctok = jnp.concatenate([jnp.tile(ca, (1, 2)), jnp.tile(cb, (1, 8)), _pad_cols(cr, 128)], axis=-1)
    stok = jnp.concatenate([jnp.tile(sa, (1, 2)), jnp.tile(sb, (1, 8)), _pad_cols(sr, 128)], axis=-1)
    cat, sat = _rope_full(S, C, HEAD_DIM, True)
    cbt, sbt = _rope_full(S, C, DIFF_QK_DIM, True)
    crt, srt = _rope_full(S, C, MLA_ROPE, True)
    one = jnp.ones((MLA_NOPE, T), F32)
    zero = jnp.zeros((MLA_NOPE, T), F32)
    padz = jnp.zeros((LANES - MLA_NOPE - MLA_ROPE, T), F32)
    cmla = jnp.tile(jnp.concatenate([one, crt, padz], axis=0), (MLA_HEADS, 1))
    smla = jnp.tile(jnp.concatenate([zero, srt, padz], axis=0), (MLA_HEADS, 1))
    ctrn = jnp.concatenate([jnp.tile(cat, (4, 1)), jnp.tile(cbt, (8, 1)), cmla], axis=0)
    strn = jnp.concatenate([jnp.tile(sat, (4, 1)), jnp.tile(sbt, (8, 1)), smla], axis=0)
    return dict(ctok=ctok, stok=stok, ctrn=ctrn, strn=strn)


def kernel(x, c, ctx, c_ctx, w_ada, b_ada, norm1_g, norm2_g, w_in, w_out, swa_sinks, diff_lq1, diff_lk1, diff_lq2, diff_lk2, diff_norm_g, mla_q_norm_g, mla_kv_norm_g, mla_w_uq, mla_w_ukv, gdn_conv_w, gdn_a_log_f, gdn_a_log_b, gdn_dt_bias_f, gdn_dt_bias_b, gdn_norm_g, moe_w_router, moe_bias, moe_w_gate, moe_w_up, moe_w_down, shared_w_gate, shared_w_up, shared_w_down, final_norm_g):
    B, S, D = x.shape
    C = ctx.shape[1]
    L = w_ada.shape[0]
    assert S % (2 * TILE) == 0 and C == TILE and B + 1 <= 8 and S % GRID_W == 0
    T = S + C
    nct = C // TILE

    xu = jnp.concatenate([ctx, x], axis=1)
    s8 = jnp.zeros((8, D), F32).at[0].set(c_ctx).at[1:B + 1].set(c)
    mod_all = _ada_call(s8, w_ada, b_ada).reshape(L, 8, 6, D)
    tabs = _rope_tables(S, C)
    gdn_e, gdn_masks = _gdn_consts()

    for l in range(L):
        last = l == L - 1
        lam_init = 0.8 - 0.6 * math.exp(-0.3 * l)
        mod8 = mod_all[l]
        lw = _layer_weights(w_in[l], mla_q_norm_g[l], mla_kv_norm_g[l], mla_w_uq[l], mla_w_ukv[l])
        (ka, qta, vta, kb, qtb, vtb, kc, qtc, vtc, dqkv, dz, dg) = _in_call(
            xu, mod8, norm1_g[l][None, :], lw, tabs, nct)

        sink_b = jnp.broadcast_to(jnp.pad(swa_sinks[l] * LOG2E, (0, 8 - SWA_HEADS))[:, None], (8, TILE))
        ya = _swa_call(qta, ka, vta, sink_b)
        g_b = jnp.broadcast_to(diff_norm_g[l][:, None], (DIFF_V_DIM, TILE))
        yb = _diff_call(qtb, kb, vtb, diff_lq1[l][None, :], diff_lk1[l][None, :], diff_lq2[l][None, :],
                        diff_lk2[l][None, :], g_b, lam_init, nct)
        yc = _mla_call(qtc, kc, vtc, nct)

        conv_w8 = jnp.pad(gdn_conv_w[l], ((0, 8 - GDN_CONV), (0, 0)))
        alog128 = jnp.pad(jnp.concatenate([gdn_a_log_f[l], gdn_a_log_b[l]]), (0, 128 - 2 * GDN_HEADS))[None, :]
        dtb128 = jnp.pad(jnp.concatenate([gdn_dt_bias_f[l], gdn_dt_bias_b[l]]), (0, 128 - 2 * GDN_HEADS))[None, :]
        gq, gk, gv, gb = _gdn_prep_call(dqkv, dg, conv_w8, alog128, dtb128, nct)
        of, ob = _gdn_scan_call(_gdn_chunk_call(gq, gk, gv, gb, gdn_e, gdn_masks), gdn_masks, nct)

        gd = jnp.tile(gdn_norm_g[l], GDN_HEADS)[None, :]
        x1, h2, wt = _out_call(ya, yb, yc, of, ob, dz, xu, mod8, gd, w_out[l].astype(BF16), norm2_g[l][None, :],
                               moe_w_router[l].T, moe_bias[l][:, None], C)
        xu = _moe_call(x1, h2, wt, mod8, l, moe_w_gate, moe_w_up, moe_w_down, shared_w_gate, shared_w_up,
                       shared_w_down, final_norm_g[None, :], last, C)
    return xu[:, C:, :]
```
